```python
import jax
import jax.numpy as jnp
from jax import lax

D_MODEL = 1024
BATCH = 16
SEQ = 2048
DEPTH = 2
DEC_BATCH = 128
DEC_SEQ = 8
PAST_LEN = 8192
PAGE_SIZE = 128

N_HEADS = D_MODEL // 64
QK_NOPE = 64
QK_ROPE = 32
V_HEAD = 64
KV_LORA = D_MODEL // 4
Q_LORA = 3 * D_MODEL // 4
LATENT = KV_LORA + QK_ROPE
ROPE_THETA = 10000.0
ATTN_SCALE = (QK_NOPE + QK_ROPE) ** -0.5
Q_BLOCK = 128
D_POOL = D_MODEL
POOL_WINDOWS = (2, 4, 8, 16)
N_POOL_GROUPS = 4
POOL_GW = D_POOL // N_POOL_GROUPS
POOL_CTX = 15
D_FF = ((8 * D_MODEL // 3 + 127) // 128) * 128
CONV_W = 3
N_IN = Q_LORA + KV_LORA + QK_ROPE + D_POOL + 2 * D_MODEL
EPS = 1e-6

kernel_name = 'hybrid_pool_mla_convffn_decoder_step'


def _rmsnorm(x, g):
    x32 = x.astype(jnp.float32)
    y = x32 * lax.rsqrt(jnp.mean(x32 * x32, axis=-1, keepdims=True) + EPS) * g.astype(jnp.float32)
    return y.astype(x.dtype)


def _rope(x, pos):
    half = x.shape[-1] // 2
    freq = ROPE_THETA ** (-jnp.arange(half, dtype=jnp.float32) / half)
    ang = pos[:, None] * freq[None, :]
    shape = (1, x.shape[1]) + (1,) * (x.ndim - 3) + (half,)
    cos = jnp.cos(ang).reshape(shape)
    sin = jnp.sin(ang).reshape(shape)
    x32 = x.astype(jnp.float32)
    x1, x2 = x32[..., :half], x32[..., half:]
    return jnp.concatenate([x1 * cos - x2 * sin, x1 * sin + x2 * cos], axis=-1).astype(x.dtype)


def _causal_attention(q, k, v, q_pos, k_pos, s_spec, o_spec):
    B, Sq = q.shape[:2]
    blk = Q_BLOCK if Sq % Q_BLOCK == 0 else Sq
    nb = Sq // blk
    qb = jnp.moveaxis(q.reshape((B, nb, blk) + q.shape[2:]), 1, 0)
    pb = q_pos.reshape(nb, blk)

    def one_block(args):
        qi, pi = args
        s = jnp.einsum(s_spec, qi, k).astype(jnp.float32) * ATTN_SCALE
        s = jnp.where(k_pos[None, None, None, :] <= pi[None, None, :, None], s, -1e30)
        p = jax.nn.softmax(s, axis=-1).astype(v.dtype)
        return jnp.einsum(o_spec, p, v)

    o = jnp.moveaxis(lax.map(one_block, (qb, pb)), 0, 1)
    return o.reshape((B, Sq) + o.shape[3:])


def _pool_mixer(u, prev, pos, w_pool, pool_scale):
    B, S, _ = u.shape
    P = prev.shape[1]
    ext = jnp.concatenate([prev, u], axis=1)
    cs = jnp.cumsum(ext.astype(jnp.float32), axis=1)
    cs = jnp.concatenate([jnp.zeros((B, 1, D_POOL), jnp.float32), cs], axis=1)
    ys = []
    for g, w in enumerate(POOL_WINDOWS):
        sl = slice(g * POOL_GW, (g + 1) * POOL_GW)
        hi = cs[:, P + 1:P + S + 1, sl]
        lo = cs[:, P + 1 - w:P + S + 1 - w, sl]
        cnt = jnp.minimum(pos + 1.0, float(w))
        ys.append((hi - lo) / cnt[None, :, None] - u[..., sl].astype(jnp.float32))
    y = jnp.stack(ys, axis=2).astype(u.dtype)
    z = jnp.einsum('bsgi,gio->bsgo', y, w_pool).reshape(B, S, D_POOL)
    return z * pool_scale, ext[:, -POOL_CTX:]


def _causal_dwconv(z, prev, w_conv, b_conv):
    S = z.shape[1]
    ext = jnp.concatenate([prev, z], axis=1)
    out = b_conv + sum(ext[:, t:t + S] * w_conv[t] for t in range(CONV_W))
    return out, ext[:, -(CONV_W - 1):]


def _layer(x, c, pos0, past, pool_prev, conv_prev, lw):
    B, S, _ = x.shape
    pos_i = pos0 + jnp.arange(S, dtype=jnp.int32)
    pos_f = pos_i.astype(jnp.float32)
    mod = c @ lw['w_ada'] + lw['b_ada']
    sh1, sc1, g1, sh2, sc2, g2 = [m[:, None, :] for m in jnp.split(mod, 6, axis=-1)]

    h = _rmsnorm(x, lw['g_attn']) * (1 + sc1) + sh1
    proj = h @ lw['w_in']
    o1 = Q_LORA
    o2 = o1 + KV_LORA
    o3 = o2 + QK_ROPE
    o4 = o3 + D_POOL
    cq, ckv, kr, u, gates = proj[..., :o1], proj[..., o1:o2], proj[..., o2:o3], proj[..., o3:o4], proj[..., o4:]

    q = (_rmsnorm(cq, lw['g_q']) @ lw['w_uq']).reshape(B, S, N_HEADS, QK_NOPE + QK_ROPE)
    q_nope = q[..., :QK_NOPE]
    q_rope = _rope(q[..., QK_NOPE:], pos_f)
    latent = jnp.concatenate([_rmsnorm(ckv, lw['g_kv']), _rope(kr, pos_f)], axis=-1)
    if past is None:
        c_lat = latent[..., :KV_LORA]
        k_nope = jnp.einsum('bkc,chn->bkhn', c_lat, lw['w_uk'])
        k_rope = jnp.broadcast_to(latent[:, :, None, KV_LORA:], (B, S, N_HEADS, QK_ROPE))
        k = jnp.concatenate([k_nope, k_rope], axis=-1)
        v = jnp.einsum('bkc,chv->bkhv', c_lat, lw['w_uv'])
        o = _causal_attention(jnp.concatenate([q_nope, q_rope], axis=-1), k, v, pos_i, pos_i,
                              'bqhd,bkhd->bhqk', 'bhqk,bkhd->bqhd')
    else:
        keys = jnp.concatenate([past, latent], axis=1)
        k_pos = jnp.arange(keys.shape[1], dtype=jnp.int32)
        q_lat = jnp.einsum('bqhn,chn->bqhc', q_nope, lw['w_uk'])
        o_lat = _causal_attention(jnp.concatenate([q_lat, q_rope], axis=-1), keys, keys[..., :KV_LORA],
                                  pos_i, k_pos, 'bqhd,bkd->bhqk', 'bhqk,bkd->bqhd')
        o = jnp.einsum('bqhc,chv->bqhv', o_lat, lw['w_uv'])
    b_out = o.reshape(B, S, N_HEADS * V_HEAD)

    a_out, pool_new = _pool_mixer(u, pool_prev, pos_f, lw['w_pool'], lw['pool_scale'])

    gate = jax.nn.sigmoid(gates)
    mixed = (gate[..., :D_MODEL] * a_out + gate[..., D_MODEL:] * b_out) @ lw['w_out']
    x = x + g1 * mixed

    h2 = _rmsnorm(x, lw['g_ffn']) * (1 + sc2) + sh2
    z, conv_new = _causal_dwconv(h2 @ lw['w_up'], conv_prev, lw['w_conv'], lw['b_conv'])
    zg, zv = z[..., :D_FF], z[..., D_FF:]
    x = x + g2 * ((jax.nn.silu(zg) * zv) @ lw['w_down'])
    return x, latent, pool_new, conv_new


def setup_inputs(seed: int = 0) -> dict:
    key = jax.random.key(seed)
    ks = iter(list(jax.random.split(key, 32)))

    def nrm(shape, scale=1.0):
        return jax.random.normal(next(ks), shape, jnp.float32) * scale

    def gain(shape):
        return 1.0 + nrm(shape, 0.01)

    n_pages = PAST_LEN // PAGE_SIZE
    n_used = DEC_BATCH * n_pages
    n_phys = n_used + n_used // 4
    return {
        'x_prompt': nrm((BATCH, SEQ, D_MODEL)),
        'x_sample': nrm((DEC_BATCH, DEC_SEQ, D_MODEL)),
        'cache_latent': nrm((DEPTH, n_phys, PAGE_SIZE, LATENT)),
        'state_pool': nrm((DEPTH, DEC_BATCH, POOL_CTX, D_POOL)),
        'state_conv': nrm((DEPTH, DEC_BATCH, CONV_W - 1, 2 * D_FF)),
        'page_table': jax.random.permutation(next(ks), n_phys)[:n_used].reshape(DEC_BATCH, n_pages).astype(jnp.int32),
        'c_prompt': nrm((BATCH, D_MODEL)),
        'c_sample': nrm((DEC_BATCH, D_MODEL)),
        'w_ada': nrm((DEPTH, D_MODEL, 6 * D_MODEL), 0.5 * D_MODEL ** -0.5),
        'b_ada': nrm((DEPTH, 6 * D_MODEL), 0.01),
        'g_attn': gain((DEPTH, D_MODEL)),
        'w_in': nrm((DEPTH, D_MODEL, N_IN), D_MODEL ** -0.5),
        'g_q': gain((DEPTH, Q_LORA)),
        'w_uq': nrm((DEPTH, Q_LORA, N_HEADS * (QK_NOPE + QK_ROPE)), Q_LORA ** -0.5),
        'g_kv': gain((DEPTH, KV_LORA)),
        'w_uk': nrm((DEPTH, KV_LORA, N_HEADS, QK_NOPE), KV_LORA ** -0.5),
        'w_uv': nrm((DEPTH, KV_LORA, N_HEADS, V_HEAD), KV_LORA ** -0.5),
        'w_pool': nrm((DEPTH, N_POOL_GROUPS, POOL_GW, POOL_GW), POOL_GW ** -0.5),
        'pool_scale': gain((DEPTH, D_POOL)),
        'w_out': nrm((DEPTH, D_MODEL, D_MODEL), D_MODEL ** -0.5),
        'g_ffn': gain((DEPTH, D_MODEL)),
        'w_up': nrm((DEPTH, D_MODEL, 2 * D_FF), D_MODEL ** -0.5),
        'w_conv': nrm((DEPTH, CONV_W, 2 * D_FF), CONV_W ** -0.5),
        'b_conv': nrm((DEPTH, 2 * D_FF), 0.01),
        'w_down': nrm((DEPTH, D_FF, D_MODEL), D_FF ** -0.5),
        'g_final': gain((D_MODEL,)),
    }


def reference(x_prompt, x_sample, cache_latent, state_pool, state_conv, page_table, c_prompt, c_sample,
              w_ada, b_ada, g_attn, w_in, g_q, w_uq, g_kv, w_uk, w_uv, w_pool, pool_scale, w_out,
              g_ffn, w_up, w_conv, b_conv, w_down, g_final):
    n_pages = PAST_LEN // PAGE_SIZE
    n_seq = x_sample.shape[0]
    yp, ys = x_prompt, x_sample
    lat_p, pool_p, conv_p, lat_s, pool_s, conv_s = [], [], [], [], [], []
    for l in range(DEPTH):
        lw = dict(w_ada=w_ada[l], b_ada=b_ada[l], g_attn=g_attn[l], w_in=w_in[l], g_q=g_q[l], w_uq=w_uq[l],
                  g_kv=g_kv[l], w_uk=w_uk[l], w_uv=w_uv[l], w_pool=w_pool[l], pool_scale=pool_scale[l],
                  w_out=w_out[l], g_ffn=g_ffn[l], w_up=w_up[l], w_conv=w_conv[l], b_conv=b_conv[l],
                  w_down=w_down[l])
        pool0 = jnp.zeros((yp.shape[0], POOL_CTX, D_POOL), yp.dtype)
        conv0 = jnp.zeros((yp.shape[0], CONV_W - 1, 2 * D_FF), yp.dtype)
        yp, lp, pp, cp = _layer(yp, c_prompt, 0, None, pool0, conv0, lw)
        past = cache_latent[l][page_table].reshape(n_seq, n_pages * PAGE_SIZE, LATENT)
        ys, ls, ps, cs = _layer(ys, c_sample, PAST_LEN, past, state_pool[l], state_conv[l], lw)
        lat_p.append(lp)
        pool_p.append(pp)
        conv_p.append(cp)
        lat_s.append(ls)
        pool_s.append(ps)
        conv_s.append(cs)
    y_prompt = _rmsnorm(yp, g_final)
    y_sample = _rmsnorm(ys, g_final)
    return (y_prompt, y_sample, jnp.stack(lat_p), jnp.stack(pool_p), jnp.stack(conv_p),
            jnp.stack(lat_s), jnp.stack(pool_s), jnp.stack(conv_s))
```

```python
import functools

import jax
import jax.numpy as jnp
from jax import lax
from jax.experimental import pallas as pl
from jax.experimental.pallas import tpu as pltpu

F32 = jnp.float32
BF16 = jnp.bfloat16

D_MODEL = 1024
N_HEADS = 16
QK_NOPE = 64
QK_ROPE = 32
ROPE_HALF = QK_ROPE // 2
V_HEAD = 64
KV_LORA = 256
Q_LORA = 768
LATENT = KV_LORA + QK_ROPE
ROPE_THETA = 10000.0
ATTN_SCALE = (QK_NOPE + QK_ROPE) ** -0.5
POOL_WINDOWS = (2, 4, 8, 16)
POOL_GW = D_MODEL // len(POOL_WINDOWS)
POOL_CTX = 15
D_FF = 2816
CONV_W = 3
PAGE_SIZE = 128
EPS = 1e-6
NEG_INF = -1e30

LANES = 128
SUBLANES = 8
HEAD_BLOCK = LANES
POOL_HALO = 16
IN_A = Q_LORA + KV_LORA
IN_KR = IN_A + LANES
IN_U = IN_KR + D_MODEL
IN_END = IN_U + 2 * D_MODEL
VMEM_LIMIT = 56 * 1024 * 1024

NT_DIMS = (((1,), (1,)), ((), ()))


def _dot(a, b):
    return jnp.dot(a, b, preferred_element_type=F32)


def _dot_nt(a, b):
    return lax.dot_general(a, b, NT_DIMS, preferred_element_type=F32)


def _rms(x, g):
    return x * lax.rsqrt(jnp.mean(x * x, axis=-1, keepdims=True) + EPS) * g


def _rows(ref, bb, ts):
    v = ref[...]
    n = v.shape[-1]
    if bb == 1:
        return v[0]
    return jnp.broadcast_to(v, (bb, ts, n)).reshape(bb * ts, n)


def _table_rows(tab_ref, idx, bb, ts):
    t = tab_ref[idx]
    if bb == 1:
        return t
    return jnp.broadcast_to(t[None], (bb, ts, LANES)).reshape(bb * ts, LANES)


def _rope_block(blk, t0, t1, t2):
    return blk * t0 + pltpu.roll(blk, ROPE_HALF, 1) * t1 + pltpu.roll(blk, LANES - ROPE_HALF, 1) * t2


def _const_spec(shape):
    nd = len(shape)
    return pl.BlockSpec(shape, lambda *_: (0,) * nd, pipeline_mode=pl.Buffered(1))


def _params(*sem):
    return pltpu.CompilerParams(dimension_semantics=sem, vmem_limit_bytes=VMEM_LIMIT)


def _ada_kernel(c_ref, w_ref, b_ref, o_ref):
    o_ref[...] = _dot(c_ref[...].astype(BF16), w_ref[...]) + b_ref[...]


def _ada(c_all, w_ada, b_ada):
    depth, d, n = w_ada.shape
    rows = c_all.shape[0]
    tn = D_MODEL
    return pl.pallas_call(
        _ada_kernel,
        grid=(depth, n // tn),
        in_specs=[pl.BlockSpec((rows, d), lambda l, j: (0, 0)),
                  pl.BlockSpec((None, d, tn), lambda l, j: (l, 0, j)),
                  pl.BlockSpec((None, 1, tn), lambda l, j: (l, 0, j))],
        out_specs=pl.BlockSpec((None, rows, tn), lambda l, j: (l, 0, j)),
        out_shape=jax.ShapeDtypeStruct((depth, rows, n), F32),
        compiler_params=_params("parallel", "parallel"),
        name="ada",
    )(c_all, w_ada, b_ada)


def _in_proj_kernel(*refs, bb, ts, prompt):
    if prompt:
        (x_ref, sh_ref, sc_ref, tab_ref, ga_ref, gq_ref, gkv_ref, win_ref, wk_ref, wv_ref,
         cqn_ref, lat_ref, u_ref, gate_ref, k_ref, v_ref) = refs
    else:
        (x_ref, sh_ref, sc_ref, tab_ref, ga_ref, gq_ref, gkv_ref, win_ref,
         cqn_ref, lat_ref, u_ref, gate_ref) = refs
    x = x_ref[...]
    h = _rms(x, ga_ref[...]) * (1.0 + _rows(sc_ref, bb, ts)) + _rows(sh_ref, bb, ts)
    hb = h.astype(BF16)
    a = _dot(hb, win_ref[:, 0:IN_A])
    cqn_ref[...] = _rms(a[:, 0:Q_LORA], gq_ref[...]).astype(BF16)
    lat = _rms(a[:, Q_LORA:IN_A], gkv_ref[...])
    lat_ref[:, 0:KV_LORA] = lat
    kr = _dot(hb, win_ref[:, IN_A:IN_KR])
    krr = _rope_block(kr, _table_rows(tab_ref, 0, bb, ts), _table_rows(tab_ref, 1, bb, ts),
                      _table_rows(tab_ref, 2, bb, ts))
    lat_ref[:, KV_LORA:LATENT] = krr[:, 0:QK_ROPE]
    u_ref[...] = _dot(hb, win_ref[:, IN_KR:IN_U])
    gate_ref[...] = jax.nn.sigmoid(_dot(hb, win_ref[:, IN_U:IN_END]))
    if prompt:
        latb = lat.astype(BF16)
        lane = lax.broadcasted_iota(jnp.int32, krr.shape, 1)
        kr_head = jnp.where(lane >= QK_NOPE, krr, 0.0)
        kf = _dot(latb, wk_ref[...])
        for hd in range(N_HEADS):
            sl = slice(hd * HEAD_BLOCK, (hd + 1) * HEAD_BLOCK)
            k_ref[:, sl] = (kf[:, sl] + kr_head).astype(BF16)
        v_ref[...] = _dot(latb, wv_ref[...]).astype(BF16)


def _in_proj(x, mod, tab, lw, *, n_batch, seq, bb, ts, prompt):
    m = x.shape[0]
    tm = bb * ts
    nst = seq // ts
    if prompt:
        row = lambda i: (i, 0)
        modmap = lambda k: (lambda i: (i // nst, 0, k))
        tabmap = lambda i: (0, i % nst, 0)
    else:
        row = lambda i: (i, 0)
        modmap = lambda k: (lambda i: (i, 0, k))
        tabmap = lambda i: (0, 0, 0)
    in_specs = [pl.BlockSpec((tm, D_MODEL), row),
                pl.BlockSpec((bb, 1, D_MODEL), modmap(0)),
                pl.BlockSpec((bb, 1, D_MODEL), modmap(1)),
                pl.BlockSpec((3, ts, LANES), tabmap),
                _const_spec((1, D_MODEL)), _const_spec((1, Q_LORA)), _const_spec((1, KV_LORA)),
                _const_spec((D_MODEL, IN_END))]
    args = [x, mod, mod, tab, lw["g_attn"], lw["g_q"], lw["g_kv"], lw["w_in"]]
    out_specs = [pl.BlockSpec((tm, Q_LORA), row), pl.BlockSpec((tm, LATENT), row),
                 pl.BlockSpec((tm, D_MODEL), row), pl.BlockSpec((tm, 2 * D_MODEL), row)]
    out_shape = [jax.ShapeDtypeStruct((m, Q_LORA), BF16), jax.ShapeDtypeStruct((m, LATENT), F32),
                 jax.ShapeDtypeStruct((m, D_MODEL), F32), jax.ShapeDtypeStruct((m, 2 * D_MODEL), F32)]
    if prompt:
        in_specs += [_const_spec((KV_LORA, N_HEADS * HEAD_BLOCK)), _const_spec((KV_LORA, N_HEADS * V_HEAD))]
        args += [lw["w_k"], lw["w_v"]]
        out_specs += [pl.BlockSpec((tm, N_HEADS * HEAD_BLOCK), row), pl.BlockSpec((tm, N_HEADS * V_HEAD), row)]
        out_shape += [jax.ShapeDtypeStruct((m, N_HEADS * HEAD_BLOCK), BF16),
                      jax.ShapeDtypeStruct((m, N_HEADS * V_HEAD), BF16)]
    return pl.pallas_call(
        functools.partial(_in_proj_kernel, bb=bb, ts=ts, prompt=prompt),
        grid=(m // tm,), in_specs=in_specs, out_specs=out_specs, out_shape=out_shape,
        compiler_params=_params("parallel"),
        name="in_proj_prompt" if prompt else "in_proj_sample",
    )(*args)


def _q_proj_kernel(cqn_ref, tab_ref, wq_ref, q_ref, *, bb, ts):
    q = _dot(cqn_ref[...], wq_ref[...])
    t0, t1, t2 = (_table_rows(tab_ref, i, bb, ts) for i in range(3))
    for hd in range(N_HEADS):
        sl = slice(hd * HEAD_BLOCK, (hd + 1) * HEAD_BLOCK)
        q_ref[:, sl] = _rope_block(q[:, sl], t0, t1, t2).astype(BF16)


def _q_proj(cqn, tab, w_q, *, seq, bb, ts, prompt):
    m = cqn.shape[0]
    tm = bb * ts
    nst = seq // ts
    tabmap = (lambda i: (0, i % nst, 0)) if prompt else (lambda i: (0, 0, 0))
    return pl.pallas_call(
        functools.partial(_q_proj_kernel, bb=bb, ts=ts),
        grid=(m // tm,),
        in_specs=[pl.BlockSpec((tm, Q_LORA), lambda i: (i, 0)),
                  pl.BlockSpec((3, ts, LANES), tabmap),
                  _const_spec((Q_LORA, N_HEADS * HEAD_BLOCK))],
        out_specs=pl.BlockSpec((tm, N_HEADS * HEAD_BLOCK), lambda i: (i, 0)),
        out_shape=jax.ShapeDtypeStruct((m, N_HEADS * HEAD_BLOCK), BF16),
        compiler_params=_params("parallel"),
        name="q_proj_prompt" if prompt else "q_proj_sample",
    )(cqn, tab, w_q)


HEADS_PER_STEP = 2


def _attn_kernel(q_ref, k_ref, v_ref, o_ref, *, tq):
    i = pl.program_id(2)
    qs = [q_ref[0, :, hh * HEAD_BLOCK:(hh + 1) * HEAD_BLOCK] for hh in range(HEADS_PER_STEP)]
    rows = lax.broadcasted_iota(jnp.int32, (tq, tq), 0)
    cols = lax.broadcasted_iota(jnp.int32, (tq, tq), 1)

    def step(j, carry, masked):
        start = pl.multiple_of(j * tq, tq)
        kj = k_ref[0, pl.ds(start, tq), :]
        vj = v_ref[0, pl.ds(start, tq), :]
        out = []
        for hh in range(HEADS_PER_STEP):
            m, l, acc = carry[hh]
            s = _dot_nt(qs[hh], kj[:, hh * HEAD_BLOCK:(hh + 1) * HEAD_BLOCK]) * ATTN_SCALE
            if masked:
                s = jnp.where(cols <= rows, s, NEG_INF)
            m_new = jnp.maximum(m, jnp.max(s, axis=-1, keepdims=True))
            alpha = jnp.exp(m - m_new)
            p = jnp.exp(s - m_new)
            l = alpha * l + jnp.sum(p, axis=-1, keepdims=True)
            acc = alpha * acc + _dot(p.astype(BF16), vj)
            out.append((m_new, l, acc))
        return tuple(out)

    init = tuple((jnp.full((tq, 1), -jnp.inf, F32), jnp.zeros((tq, 1), F32),
                  jnp.zeros((tq, HEADS_PER_STEP * V_HEAD), F32)) for _ in range(HEADS_PER_STEP))
    carry = lax.fori_loop(0, i, lambda j, c: step(j, c, False), init)
    carry = step(i, carry, True)
    lane = lax.broadcasted_iota(jnp.int32, (tq, HEADS_PER_STEP * V_HEAD), 1)
    o0 = carry[0][2] / carry[0][1]
    o1 = carry[1][2] / carry[1][1]
    o_ref[0] = jnp.where(lane < V_HEAD, o0, o1)


def _attn(q, k, v, *, tq):
    b, s, _ = q.shape
    hw = HEADS_PER_STEP * HEAD_BLOCK
    vw = HEADS_PER_STEP * V_HEAD
    return pl.pallas_call(
        functools.partial(_attn_kernel, tq=tq),
        grid=(b, N_HEADS // HEADS_PER_STEP, s // tq),
        in_specs=[pl.BlockSpec((1, tq, hw), lambda bi, hp, i: (bi, i, hp)),
                  pl.BlockSpec((1, s, hw), lambda bi, hp, i: (bi, 0, hp)),
                  pl.BlockSpec((1, s, vw), lambda bi, hp, i: (bi, 0, hp))],
        out_specs=pl.BlockSpec((1, tq, vw), lambda bi, hp, i: (bi, i, hp)),
        out_shape=jax.ShapeDtypeStruct((b, s, N_HEADS * V_HEAD), F32),
        compiler_params=_params("parallel", "parallel", "arbitrary"),
        name="attn_prompt",
    )(q, k, v)


def _absorb_kernel(q_ref, w_ref, qlat_ref, qrope_ref, *, n_seq, ts):
    blk = q_ref[...]
    qlat = _dot(blk, w_ref[0])
    qlat_ref[...] = qlat.reshape(n_seq, ts, KV_LORA)
    rolled = pltpu.roll(blk.astype(F32), LANES - QK_NOPE, 1)
    qrope_ref[...] = rolled[:, 0:QK_ROPE].reshape(n_seq, ts, QK_ROPE)


def _absorb(q, w_ukt, *, n_seq, ts):
    m = q.shape[0]
    return pl.pallas_call(
        functools.partial(_absorb_kernel, n_seq=n_seq, ts=ts),
        grid=(N_HEADS,),
        in_specs=[pl.BlockSpec((m, HEAD_BLOCK), lambda h: (0, h)),
                  pl.BlockSpec((1, HEAD_BLOCK, KV_LORA), lambda h: (h, 0, 0))],
        out_specs=[pl.BlockSpec((n_seq, None, ts, KV_LORA), lambda h: (0, h, 0, 0)),
                   pl.BlockSpec((n_seq, None, ts, QK_ROPE), lambda h: (0, h, 0, 0))],
        out_shape=[jax.ShapeDtypeStruct((n_seq, N_HEADS, ts, KV_LORA), F32),
                   jax.ShapeDtypeStruct((n_seq, N_HEADS, ts, QK_ROPE), F32)],
        compiler_params=_params("parallel"),
        name="absorb",
    )(q, w_ukt)


def _paged_attn_kernel(pt_ref, qlat_ref, qrope_ref, new_ref, *rest, n_pages, ts):
    page_refs = rest[:n_pages]
    o_ref = rest[n_pages]
    kl_ref, kr_ref = rest[n_pages + 1:]
    nq = N_HEADS * ts
    for j in range(n_pages):
        pg = page_refs[j][...]
        kl_ref[j * PAGE_SIZE:(j + 1) * PAGE_SIZE, :] = pg[:, 0:KV_LORA].astype(BF16)
        kr_ref[j * PAGE_SIZE:(j + 1) * PAGE_SIZE, :] = pg[:, KV_LORA:LATENT].astype(BF16)
    ql = qlat_ref[0].reshape(nq, KV_LORA).astype(BF16)
    qr = qrope_ref[0].reshape(nq, QK_ROPE).astype(BF16)
    kl = kl_ref[...]
    s = (_dot_nt(ql, kl) + _dot_nt(qr, kr_ref[...])) * ATTN_SCALE
    new = new_ref[0]
    nl = new[:, 0:KV_LORA].astype(BF16)
    s_new = (_dot_nt(ql, nl) + _dot_nt(qr, new[:, KV_LORA:LATENT].astype(BF16))) * ATTN_SCALE
    tok = lax.broadcasted_iota(jnp.int32, (nq, ts), 0) % ts
    key = lax.broadcasted_iota(jnp.int32, (nq, ts), 1)
    s_new = jnp.where(key <= tok, s_new, NEG_INF)
    m = jnp.maximum(jnp.max(s, axis=-1, keepdims=True), jnp.max(s_new, axis=-1, keepdims=True))
    p = jnp.exp(s - m)
    p_new = jnp.exp(s_new - m)
    l = jnp.sum(p, axis=-1, keepdims=True) + jnp.sum(p_new, axis=-1, keepdims=True)
    o = (_dot(p.astype(BF16), kl) + _dot(p_new.astype(BF16), nl)) / l
    o_ref[0] = o.reshape(N_HEADS, ts, KV_LORA)


def _paged_attn(page_table, qlat, qrope, lat_new, cache, layer):
    n_seq, n_pages = page_table.shape
    ts = qlat.shape[2]

    def page_spec(j):
        return pl.BlockSpec((None, None, PAGE_SIZE, LATENT),
                            lambda b, pt: (layer, pt[b * n_pages + j], 0, 0))

    grid_spec = pltpu.PrefetchScalarGridSpec(
        num_scalar_prefetch=1,
        grid=(n_seq,),
        in_specs=[pl.BlockSpec((1, N_HEADS, ts, KV_LORA), lambda b, pt: (b, 0, 0, 0)),
                  pl.BlockSpec((1, N_HEADS, ts, QK_ROPE), lambda b, pt: (b, 0, 0, 0)),
                  pl.BlockSpec((1, ts, LATENT), lambda b, pt: (b, 0, 0))]
                 + [page_spec(j) for j in range(n_pages)],
        out_specs=pl.BlockSpec((1, N_HEADS, ts, KV_LORA), lambda b, pt: (b, 0, 0, 0)),
        scratch_shapes=[pltpu.VMEM((n_pages * PAGE_SIZE, KV_LORA), BF16),
                        pltpu.VMEM((n_pages * PAGE_SIZE, QK_ROPE), BF16)],
    )
    return pl.pallas_call(
        functools.partial(_paged_attn_kernel, n_pages=n_pages, ts=ts),
        grid_spec=grid_spec,
        out_shape=jax.ShapeDtypeStruct((n_seq, N_HEADS, ts, KV_LORA), F32),
        compiler_params=_params("arbitrary"),
        name="paged_attn",
    )(page_table.reshape(-1), qlat, qrope, lat_new, *([cache] * n_pages))


def _v_up_kernel(o_ref, w_ref, out_ref, *, n_seq, ts):
    acc = None
    for e in range(HEADS_PER_STEP):
        x = o_ref[:, e].reshape(n_seq * ts, KV_LORA).astype(BF16)
        d = _dot(x, w_ref[e])
        acc = d if acc is None else acc + d
    out_ref[...] = acc


def _v_up(olat, w_vup):
    n_seq, _, ts, _ = olat.shape
    vw = HEADS_PER_STEP * V_HEAD
    return pl.pallas_call(
        functools.partial(_v_up_kernel, n_seq=n_seq, ts=ts),
        grid=(N_HEADS // HEADS_PER_STEP,),
        in_specs=[pl.BlockSpec((n_seq, HEADS_PER_STEP, ts, KV_LORA), lambda p: (0, p, 0, 0)),
                  pl.BlockSpec((HEADS_PER_STEP, KV_LORA, vw), lambda p: (p, 0, 0))],
        out_specs=pl.BlockSpec((n_seq * ts, vw), lambda p: (0, p)),
        out_shape=jax.ShapeDtypeStruct((n_seq * ts, N_HEADS * V_HEAD), F32),
        compiler_params=_params("parallel"),
        name="v_up",
    )(olat, w_vup)


def _mix_kernel(x_ref, u_ref, prev_ref, gate_ref, o_ref, g1_ref, wpool_ref, pscale_ref, wout_ref,
                x1_ref, ext_ref, *, bb, ts, pos0, prompt):
    j = pl.program_id(1)
    tm = bb * ts
    u = u_ref[...]
    ext_ref[:, POOL_HALO:, :] = u
    if prompt:
        ext_ref[:, 0:POOL_HALO, :] = jnp.where(j > 0, prev_ref[...], 0.0)
    else:
        ext_ref[:, POOL_HALO - POOL_CTX:POOL_HALO, :] = prev_ref[...]
    pos = pos0 + j * ts + lax.broadcasted_iota(jnp.int32, (1, ts, 1), 1)
    mixed = []
    for g, w in enumerate(POOL_WINDOWS):
        cs = slice(g * POOL_GW, (g + 1) * POOL_GW)
        acc = ext_ref[:, POOL_HALO:POOL_HALO + ts, cs]
        for t in range(1, w):
            acc = acc + ext_ref[:, POOL_HALO - t:POOL_HALO - t + ts, cs]
        cnt = jnp.minimum(pos + 1, w).astype(F32)
        y = acc / cnt - u[:, :, cs]
        z = _dot(y.reshape(tm, POOL_GW).astype(BF16), wpool_ref[g]) * pscale_ref[:, cs]
        ga = gate_ref[:, :, cs].reshape(tm, POOL_GW)
        gb = gate_ref[:, :, D_MODEL + g * POOL_GW:D_MODEL + (g + 1) * POOL_GW].reshape(tm, POOL_GW)
        mixed.append((ga * z + gb * o_ref[:, :, cs].reshape(tm, POOL_GW)).astype(BF16))
    out = _dot(jnp.concatenate(mixed, axis=-1), wout_ref[...])
    x1 = x_ref[...].reshape(tm, D_MODEL) + _rows(g1_ref, bb, ts) * out
    x1_ref[...] = x1.reshape(bb, ts, D_MODEL)


def _mix(x, u, prev, gate, o, mod, lw, *, bb, ts, pos0, prompt):
    b, s, _ = x.shape
    tok = lambda n: pl.BlockSpec((bb, ts, n), lambda bi, j: (bi, j, 0))
    if prompt:
        r = ts // POOL_HALO
        prev_spec = pl.BlockSpec((bb, POOL_HALO, D_MODEL), lambda bi, j: (bi, jnp.maximum(j * r - 1, 0), 0))
    else:
        prev_spec = pl.BlockSpec((bb, POOL_CTX, D_MODEL), lambda bi, j: (bi, 0, 0))
    return pl.pallas_call(
        functools.partial(_mix_kernel, bb=bb, ts=ts, pos0=pos0, prompt=prompt),
        grid=(b // bb, s // ts),
        in_specs=[tok(D_MODEL), tok(D_MODEL), prev_spec, tok(2 * D_MODEL), tok(D_MODEL),
                  pl.BlockSpec((bb, 1, D_MODEL), lambda bi, j: (bi, 0, 2)),
                  _const_spec((len(POOL_WINDOWS), POOL_GW, POOL_GW)), _const_spec((1, D_MODEL)),
                  _const_spec((D_MODEL, D_MODEL))],
        out_specs=tok(D_MODEL),
        out_shape=jax.ShapeDtypeStruct((b, s, D_MODEL), F32),
        scratch_shapes=[pltpu.VMEM((bb, POOL_HALO + ts, D_MODEL), F32)],
        compiler_params=_params("parallel", "arbitrary"),
        name="mix_prompt" if prompt else "mix_sample",
    )(x, u, prev, gate, o, mod, lw["w_pool"], lw["pool_scale"], lw["w_out"])


FF_CHUNK = D_FF // 2
CONV_HALO = SUBLANES


def _ffn_kernel(x_ref, prev_ref, sh_ref, sc_ref, g2_ref, gf_ref, wup_ref, wconv_ref, bconv_ref, wdown_ref,
                gfin_ref, y_ref, cnew_ref, extg_ref, extv_ref, *, bb, ts, prompt, final):
    j = pl.program_id(1)
    tm = bb * ts
    x = x_ref[...].reshape(tm, D_MODEL)
    sc = _rows(sc_ref, bb, ts)
    sh = _rows(sh_ref, bb, ts)
    if prompt:
        xe = jnp.concatenate([prev_ref[0], x], axis=0)
    else:
        xe = x
    hb = (_rms(xe, gf_ref[...]) * (1.0 + sc) + sh).astype(BF16)
    acc = jnp.zeros((tm, D_MODEL), F32)
    for c in range(D_FF // FF_CHUNK):
        conv = []
        for part, ext_ref in ((0, extg_ref), (1, extv_ref)):
            lo = part * D_FF + c * FF_CHUNK
            cs = slice(lo, lo + FF_CHUNK)
            z = _dot(hb, wup_ref[:, cs])
            if prompt:
                keep = jnp.logical_or(j > 0, lax.broadcasted_iota(jnp.int32, (tm + CONV_HALO, 1), 0) >= CONV_HALO)
                ext_ref[0] = jnp.where(keep, z, 0.0)
            else:
                ext_ref[:, CONV_HALO:, :] = z.reshape(bb, ts, FF_CHUNK)
                ext_ref[:, CONV_HALO - (CONV_W - 1):CONV_HALO, :] = prev_ref[:, :, cs]
            out = bconv_ref[:, cs][None]
            for t in range(CONV_W):
                lo_r = CONV_HALO - (CONV_W - 1) + t
                out = out + ext_ref[:, lo_r:lo_r + ts, :] * wconv_ref[t:t + 1, cs][None]
            cnew_ref[:, :, cs] = ext_ref[:, CONV_HALO + ts - (CONV_W - 1):CONV_HALO + ts, :]
            conv.append(out.reshape(tm, FF_CHUNK))
        act = (jax.nn.silu(conv[0]) * conv[1]).astype(BF16)
        acc = acc + _dot(act, wdown_ref[c * FF_CHUNK:(c + 1) * FF_CHUNK, :])
    x2 = x + _rows(g2_ref, bb, ts) * acc
    if final:
        x2 = _rms(x2, gfin_ref[...])
    y_ref[...] = x2.reshape(bb, ts, D_MODEL)


def _ffn(x, prev, mod, lw, g_final, *, bb, ts, prompt, final):
    b, s, _ = x.shape
    tok = pl.BlockSpec((bb, ts, D_MODEL), lambda bi, j: (bi, j, 0))
    if prompt:
        r = ts // CONV_HALO
        prev_spec = pl.BlockSpec((bb, CONV_HALO, D_MODEL), lambda bi, j: (bi, jnp.maximum(j * r - 1, 0), 0))
        halo = CONV_HALO
    else:
        prev_spec = pl.BlockSpec((bb, CONV_W - 1, 2 * D_FF), lambda bi, j: (bi, 0, 0))
        halo = 0
    modspec = lambda k: pl.BlockSpec((bb, 1, D_MODEL), lambda bi, j: (bi, 0, k))
    return pl.pallas_call(
        functools.partial(_ffn_kernel, bb=bb, ts=ts, prompt=prompt, final=final),
        grid=(b // bb, s // ts),
        in_specs=[tok, prev_spec, modspec(3), modspec(4), modspec(5), _const_spec((1, D_MODEL)),
                  _const_spec((D_MODEL, 2 * D_FF)), _const_spec((CONV_W, 2 * D_FF)), _const_spec((1, 2 * D_FF)),
                  _const_spec((D_FF, D_MODEL)), _const_spec((1, D_MODEL))],
        out_specs=[tok, pl.BlockSpec((bb, CONV_W - 1, 2 * D_FF), lambda bi, j: (bi, 0, 0))],
        out_shape=[jax.ShapeDtypeStruct((b, s, D_MODEL), F32),
                   jax.ShapeDtypeStruct((b, CONV_W - 1, 2 * D_FF), F32)],
        scratch_shapes=[pltpu.VMEM((bb, CONV_HALO + ts, FF_CHUNK), F32),
                        pltpu.VMEM((bb, CONV_HALO + ts, FF_CHUNK), F32)],
        compiler_params=_params("parallel", "arbitrary"),
        name=("ffn_prompt" if prompt else "ffn_sample") + ("_final" if final else ""),
    )(x, prev, mod, mod, mod, lw["g_ffn"], lw["w_up"], lw["w_conv"], lw["b_conv"], lw["w_down"], g_final)


def _rope_tables(pos):
    freq = ROPE_THETA ** (-jnp.arange(ROPE_HALF, dtype=F32) / ROPE_HALF)
    ang = pos[:, None] * freq[None, :]
    cos, sin = jnp.cos(ang), jnp.sin(ang)
    z16 = jnp.zeros_like(cos)
    one = jnp.ones((pos.shape[0], QK_NOPE), F32)
    zero = jnp.zeros((pos.shape[0], QK_NOPE), F32)
    pad = jnp.zeros((pos.shape[0], LANES - QK_NOPE - QK_ROPE), F32)
    span0, span1, span2 = [cos, cos], [z16, sin], [-sin, z16]
    cat = lambda parts: jnp.concatenate(parts, axis=-1)
    tab_k = jnp.stack([cat(span0 + [pad] + span0 + [pad]), cat(span1 + [pad] + span1 + [pad]),
                       cat(span2 + [pad] + span2 + [pad])])
    tab_q = jnp.stack([cat([one] + span0 + [pad]), cat([zero] + span1 + [pad]), cat([zero] + span2 + [pad])])
    return tab_k, tab_q


def _prep_layer(l, w_in, g_attn, g_q, w_uq, g_kv, w_uk, w_uv, w_pool, pool_scale, w_out, g_ffn, w_up,
                w_conv, b_conv, w_down):
    wi = w_in[l]
    o2 = IN_A
    o3 = o2 + QK_ROPE
    kr = wi[:, o2:o3]
    zpad = jnp.zeros((D_MODEL, LANES // 2 - QK_ROPE), F32)
    w_in_p = jnp.concatenate([wi[:, :o2], kr, zpad, kr, zpad, wi[:, o3:]], axis=1).astype(BF16)
    hpad = HEAD_BLOCK - QK_NOPE - QK_ROPE
    w_q = jnp.pad(w_uq[l].reshape(Q_LORA, N_HEADS, QK_NOPE + QK_ROPE), ((0, 0), (0, 0), (0, hpad)))
    w_k = jnp.pad(w_uk[l], ((0, 0), (0, 0), (0, HEAD_BLOCK - QK_NOPE)))
    w_ukt = jnp.pad(jnp.transpose(w_uk[l], (1, 2, 0)), ((0, 0), (0, HEAD_BLOCK - QK_NOPE), (0, 0)))
    wv_heads = jnp.transpose(w_uv[l], (1, 0, 2))
    w_vup = jnp.stack([jnp.pad(wv_heads[h], ((0, 0), ((h % HEADS_PER_STEP) * V_HEAD,
                                                       (HEADS_PER_STEP - 1 - h % HEADS_PER_STEP) * V_HEAD)))
                       for h in range(N_HEADS)])
    return dict(
        w_in=w_in_p, g_attn=g_attn[l][None], g_q=g_q[l][None], g_kv=g_kv[l][None],
        w_q=w_q.reshape(Q_LORA, N_HEADS * HEAD_BLOCK).astype(BF16),
        w_k=w_k.reshape(KV_LORA, N_HEADS * HEAD_BLOCK).astype(BF16),
        w_v=w_uv[l].reshape(KV_LORA, N_HEADS * V_HEAD).astype(BF16),
        w_ukt=w_ukt.astype(BF16), w_vup=w_vup.astype(BF16),
        w_pool=w_pool[l].astype(BF16), pool_scale=pool_scale[l][None], w_out=w_out[l].astype(BF16),
        g_ffn=g_ffn[l][None], w_up=w_up[l].astype(BF16), w_conv=w_conv[l], b_conv=b_conv[l][None],
        w_down=w_down[l].astype(BF16))


def _pick(n, pref):
    t = min(n, pref)
    while n % t:
        t -= 1
    return t


def kernel(x_prompt, x_sample, cache_latent, state_pool, state_conv, page_table, c_prompt, c_sample, w_ada, b_ada, g_attn, w_in, g_q, w_uq, g_kv, w_uk, w_uv, w_pool, pool_scale, w_out, g_ffn, w_up, w_conv, b_conv, w_down, g_final):
    depth = w_ada.shape[0]
    bp, sp, _ = x_prompt.shape
    bs, ss, _ = x_sample.shape
    past_len = page_table.shape[1] * PAGE_SIZE

    mod_all = _ada(jnp.concatenate([c_prompt, c_sample], axis=0), w_ada.astype(BF16), b_ada[:, None, :])
    tabk_p, tabq_p = _rope_tables(jnp.arange(sp, dtype=F32))
    tabk_s, tabq_s = _rope_tables(past_len + jnp.arange(ss, dtype=F32))
    gfin = g_final[None]

    ts_p = _pick(sp, 512)
    ts_ffn = _pick(sp, 256)
    bb_s = _pick(bs, 32)

    yp, ys = x_prompt, x_sample
    outs = [[] for _ in range(6)]
    for l in range(depth):
        lw = _prep_layer(l, w_in, g_attn, g_q, w_uq, g_kv, w_uk, w_uv, w_pool, pool_scale, w_out, g_ffn,
                         w_up, w_conv, b_conv, w_down)
        final = l == depth - 1
        mod_p = mod_all[l, :bp][:, None, :]
        mod_s = mod_all[l, bp:][:, None, :]

        cqn, lat, u, gate, kf, vf = _in_proj(yp.reshape(bp * sp, D_MODEL), mod_p, tabk_p, lw,
                                             n_batch=bp, seq=sp, bb=1, ts=ts_p, prompt=True)
        q = _q_proj(cqn, tabq_p, lw["w_q"], seq=sp, bb=1, ts=ts_p, prompt=True)
        o = _attn(q.reshape(bp, sp, -1), kf.reshape(bp, sp, -1), vf.reshape(bp, sp, -1), tq=ts_p)
        u3 = u.reshape(bp, sp, D_MODEL)
        x1 = _mix(yp, u3, u3, gate.reshape(bp, sp, -1), o, mod_p, lw, bb=1, ts=ts_p, pos0=0, prompt=True)
        yp, conv_p = _ffn(x1, x1, mod_p, lw, gfin, bb=1, ts=ts_ffn, prompt=True, final=final)
        outs[0].append(lat.reshape(bp, sp, LATENT))
        outs[1].append(u3[:, sp - POOL_CTX:, :])
        outs[2].append(conv_p)

        cqn, lat, u, gate = _in_proj(ys.reshape(bs * ss, D_MODEL), mod_s, tabk_s, lw,
                                     n_batch=bs, seq=ss, bb=bb_s, ts=ss, prompt=False)
        q = _q_proj(cqn, tabq_s, lw["w_q"], seq=ss, bb=bb_s, ts=ss, prompt=False)
        qlat, qrope = _absorb(q, lw["w_ukt"], n_seq=bs, ts=ss)
        lat3 = lat.reshape(bs, ss, LATENT)
        olat = _paged_attn(page_table, qlat, qrope, lat3, cache_latent, l)
        o = _v_up(olat, lw["w_vup"])
        u3 = u.reshape(bs, ss, D_MODEL)
        x1 = _mix(ys, u3, state_pool[l], gate.reshape(bs, ss, -1), o.reshape(bs, ss, -1), mod_s, lw,
                  bb=bb_s, ts=ss, pos0=past_len, prompt=False)
        ys, conv_s = _ffn(x1, state_conv[l], mod_s, lw, gfin, bb=bb_s, ts=ss, prompt=False, final=final)
        outs[3].append(lat3)
        outs[4].append(jnp.concatenate([state_pool[l], u3], axis=1)[:, -POOL_CTX:, :])
        outs[5].append(conv_s)

    return (yp, ys) + tuple(jnp.stack(o) for o in outs)
```

```python
import functools

import jax
import jax.numpy as jnp
from jax import lax
from jax.experimental import pallas as pl
from jax.experimental.pallas import tpu as pltpu

F32 = jnp.float32
BF16 = jnp.bfloat16

D_MODEL = 1024
N_HEADS = 16
QK_NOPE = 64
QK_ROPE = 32
ROPE_HALF = QK_ROPE // 2
V_HEAD = 64
KV_LORA = 256
Q_LORA = 768
LATENT = KV_LORA + QK_ROPE
ROPE_THETA = 10000.0
ATTN_SCALE = (QK_NOPE + QK_ROPE) ** -0.5
Q_PRESCALE = ATTN_SCALE * 1.4426950408889634
POOL_WINDOWS = (2, 4, 8, 16)
POOL_GW = D_MODEL // len(POOL_WINDOWS)
POOL_CTX = 15
D_FF = 2816
CONV_W = 3
PAGE_SIZE = 128
EPS = 1e-6
NEG_INF = -1e30

LANES = 128
SUBLANES = 8
HEAD_BLOCK = LANES
POOL_HALO = 16
IN_A = Q_LORA + KV_LORA
IN_KR = IN_A + LANES
IN_U = IN_KR + D_MODEL
IN_END = IN_U + 2 * D_MODEL
VMEM_LIMIT = 56 * 1024 * 1024

NT_DIMS = (((1,), (1,)), ((), ()))


def _dot(a, b):
    return jnp.dot(a, b, preferred_element_type=F32)


def _dot_nt(a, b):
    return lax.dot_general(a, b, NT_DIMS, preferred_element_type=F32)


def _rms(x, g):
    return x * lax.rsqrt(jnp.mean(x * x, axis=-1, keepdims=True) + EPS) * g


def _rows(ref, bb, ts):
    v = ref[...]
    n = v.shape[-1]
    if bb == 1:
        return v[0]
    return jnp.broadcast_to(v, (bb, ts, n)).reshape(bb * ts, n)


def _table_rows(tab_ref, idx, bb, ts):
    t = tab_ref[idx]
    if bb == 1:
        return t
    return jnp.broadcast_to(t[None], (bb, ts, LANES)).reshape(bb * ts, LANES)


def _rope_block(blk, t0, t1, t2):
    return blk * t0 + pltpu.roll(blk, ROPE_HALF, 1) * t1 + pltpu.roll(blk, LANES - ROPE_HALF, 1) * t2


def _const_spec(shape):
    nd = len(shape)
    return pl.BlockSpec(shape, lambda *_: (0,) * nd, pipeline_mode=pl.Buffered(1))


def _params(*sem):
    return pltpu.CompilerParams(dimension_semantics=sem, vmem_limit_bytes=VMEM_LIMIT)


def _ada_kernel(c_ref, w_ref, b_ref, o_ref):
    o_ref[...] = _dot(c_ref[...].astype(BF16), w_ref[...]) + b_ref[...]


def _ada(c_all, w_ada, b_ada):
    depth, d, n = w_ada.shape
    rows = c_all.shape[0]
    tn = D_MODEL
    return pl.pallas_call(
        _ada_kernel,
        grid=(depth, n // tn),
        in_specs=[pl.BlockSpec((rows, d), lambda l, j: (0, 0)),
                  pl.BlockSpec((None, d, tn), lambda l, j: (l, 0, j)),
                  pl.BlockSpec((None, 1, tn), lambda l, j: (l, 0, j))],
        out_specs=pl.BlockSpec((None, rows, tn), lambda l, j: (l, 0, j)),
        out_shape=jax.ShapeDtypeStruct((depth, rows, n), F32),
        compiler_params=_params("parallel", "parallel"),
        name="ada",
    )(c_all, w_ada, b_ada)


def _in_proj_kernel(*refs, bb, ts, prompt):
    if prompt:
        (x_ref, sh_ref, sc_ref, tab_ref, ga_ref, gq_ref, gkv_ref, win_ref, wk_ref, wv_ref,
         cqn_ref, lat_ref, u_ref, gate_ref, k_ref, v_ref) = refs
    else:
        (x_ref, sh_ref, sc_ref, tab_ref, ga_ref, gq_ref, gkv_ref, win_ref,
         cqn_ref, lat_ref, u_ref, gate_ref) = refs
    x = x_ref[...]
    h = _rms(x, ga_ref[...]) * (1.0 + _rows(sc_ref, bb, ts)) + _rows(sh_ref, bb, ts)
    hb = h.astype(BF16)
    a = _dot(hb, win_ref[:, 0:IN_A])
    cqn_ref[...] = _rms(a[:, 0:Q_LORA], gq_ref[...]).astype(BF16)
    lat = _rms(a[:, Q_LORA:IN_A], gkv_ref[...])
    lat_ref[:, 0:KV_LORA] = lat
    kr = _dot(hb, win_ref[:, IN_A:IN_KR])
    krr = _rope_block(kr, _table_rows(tab_ref, 0, bb, ts), _table_rows(tab_ref, 1, bb, ts),
                      _table_rows(tab_ref, 2, bb, ts))
    lat_ref[:, KV_LORA:LATENT] = krr[:, 0:QK_ROPE]
    u_ref[...] = _dot(hb, win_ref[:, IN_KR:IN_U])
    gate_ref[...] = jax.nn.sigmoid(_dot(hb, win_ref[:, IN_U:IN_END]))
    if prompt:
        latb = lat.astype(BF16)
        lane = lax.broadcasted_iota(jnp.int32, krr.shape, 1)
        kr_head = jnp.where(lane >= QK_NOPE, krr, 0.0)
        kf = _dot(latb, wk_ref[...])
        for hd in range(N_HEADS):
            sl = slice(hd * HEAD_BLOCK, (hd + 1) * HEAD_BLOCK)
            k_ref[:, sl] = (kf[:, sl] + kr_head).astype(BF16)
        v_ref[...] = _dot(latb, wv_ref[...]).astype(BF16)


def _in_proj(x, mod, tab, lw, *, n_batch, seq, bb, ts, prompt):
    m = x.shape[0]
    tm = bb * ts
    nst = seq // ts
    if prompt:
        row = lambda i: (i, 0)
        modmap = lambda k: (lambda i: (i // nst, 0, k))
        tabmap = lambda i: (0, i % nst, 0)
    else:
        row = lambda i: (i, 0)
        modmap = lambda k: (lambda i: (i, 0, k))
        tabmap = lambda i: (0, 0, 0)
    in_specs = [pl.BlockSpec((tm, D_MODEL), row),
                pl.BlockSpec((bb, 1, D_MODEL), modmap(0)),
                pl.BlockSpec((bb, 1, D_MODEL), modmap(1)),
                pl.BlockSpec((3, ts, LANES), tabmap),
                _const_spec((1, D_MODEL)), _const_spec((1, Q_LORA)), _const_spec((1, KV_LORA)),
                _const_spec((D_MODEL, IN_END))]
    args = [x, mod, mod, tab, lw["g_attn"], lw["g_q"], lw["g_kv"], lw["w_in"]]
    out_specs = [pl.BlockSpec((tm, Q_LORA), row), pl.BlockSpec((tm, LATENT), row),
                 pl.BlockSpec((tm, D_MODEL), row), pl.BlockSpec((tm, 2 * D_MODEL), row)]
    out_shape = [jax.ShapeDtypeStruct((m, Q_LORA), BF16), jax.ShapeDtypeStruct((m, LATENT), F32),
                 jax.ShapeDtypeStruct((m, D_MODEL), F32), jax.ShapeDtypeStruct((m, 2 * D_MODEL), F32)]
    if prompt:
        in_specs += [_const_spec((KV_LORA, N_HEADS * HEAD_BLOCK)), _const_spec((KV_LORA, N_HEADS * V_HEAD))]
        args += [lw["w_k"], lw["w_v"]]
        out_specs += [pl.BlockSpec((tm, N_HEADS * HEAD_BLOCK), row), pl.BlockSpec((tm, N_HEADS * V_HEAD), row)]
        out_shape += [jax.ShapeDtypeStruct((m, N_HEADS * HEAD_BLOCK), BF16),
                      jax.ShapeDtypeStruct((m, N_HEADS * V_HEAD), BF16)]
    return pl.pallas_call(
        functools.partial(_in_proj_kernel, bb=bb, ts=ts, prompt=prompt),
        grid=(m // tm,), in_specs=in_specs, out_specs=out_specs, out_shape=out_shape,
        compiler_params=_params("parallel"),
        name="in_proj_prompt" if prompt else "in_proj_sample",
    )(*args)


def _q_proj_kernel(cqn_ref, tab_ref, wq_ref, q_ref, *, bb, ts):
    q = _dot(cqn_ref[...], wq_ref[...])
    t0, t1, t2 = (_table_rows(tab_ref, i, bb, ts) for i in range(3))
    for hd in range(N_HEADS):
        sl = slice(hd * HEAD_BLOCK, (hd + 1) * HEAD_BLOCK)
        q_ref[:, sl] = _rope_block(q[:, sl], t0, t1, t2).astype(BF16)


def _q_proj(cqn, tab, w_q, *, seq, bb, ts, prompt):
    m = cqn.shape[0]
    tm = bb * ts
    nst = seq // ts
    tabmap = (lambda i: (0, i % nst, 0)) if prompt else (lambda i: (0, 0, 0))
    return pl.pallas_call(
        functools.partial(_q_proj_kernel, bb=bb, ts=ts),
        grid=(m // tm,),
        in_specs=[pl.BlockSpec((tm, Q_LORA), lambda i: (i, 0)),
                  pl.BlockSpec((3, ts, LANES), tabmap),
                  _const_spec((Q_LORA, N_HEADS * HEAD_BLOCK))],
        out_specs=pl.BlockSpec((tm, N_HEADS * HEAD_BLOCK), lambda i: (i, 0)),
        out_shape=jax.ShapeDtypeStruct((m, N_HEADS * HEAD_BLOCK), BF16),
        compiler_params=_params("parallel"),
        name="q_proj_prompt" if prompt else "q_proj_sample",
    )(cqn, tab, w_q)


HEADS_PER_STEP = 2


ATTN_STRIP = 16


def _attn_kernel(q_ref, k_ref, v_ref, o_ref, s_ref, p_ref, m_ref, l_ref, acc_ref, *, tq):
    i = pl.program_id(2)
    reps = tq // LANES
    m_ref[...] = jnp.full(m_ref.shape, -jnp.inf, F32)
    l_ref[...] = jnp.zeros(l_ref.shape, F32)
    acc_ref[...] = jnp.zeros(acc_ref.shape, F32)

    def tile(j, masked):
        start = pl.multiple_of(j * tq, tq)
        for hh in range(HEADS_PER_STEP):
            hs = slice(hh * HEAD_BLOCK, (hh + 1) * HEAD_BLOCK)
            s_ref[hh] = _dot_nt(q_ref[0, :, hs], k_ref[0, pl.ds(start, tq), hs])
        vj = v_ref[0, pl.ds(start, tq), :]
        for hh in range(HEADS_PER_STEP):
            for r in range(tq // ATTN_STRIP):
                rows = slice(r * ATTN_STRIP, (r + 1) * ATTN_STRIP)
                s = s_ref[hh, rows, :]
                if masked:
                    qi = r * ATTN_STRIP + lax.broadcasted_iota(jnp.int32, (ATTN_STRIP, tq), 0)
                    ki = lax.broadcasted_iota(jnp.int32, (ATTN_STRIP, tq), 1)
                    s = jnp.where(ki <= qi, s, NEG_INF)
                m_prev = m_ref[hh, rows, :]
                m_new = jnp.maximum(m_prev, jnp.max(s, axis=-1, keepdims=True))
                alpha = jnp.exp2(m_prev - m_new)
                p = jnp.exp2(s - jnp.tile(m_new, (1, reps)))
                l_ref[hh, rows, :] = alpha * l_ref[hh, rows, :] + jnp.sum(p, axis=-1, keepdims=True)
                m_ref[hh, rows, :] = m_new
                acc_ref[hh, rows, :] = alpha * acc_ref[hh, rows, :]
                p_ref[hh, rows, :] = p.astype(BF16)
            acc_ref[hh] += _dot(p_ref[hh], vj)

    def unmasked(j, c):
        tile(j, False)
        return c

    lax.fori_loop(0, i, unmasked, 0)
    tile(i, True)
    lane = lax.broadcasted_iota(jnp.int32, (tq, HEADS_PER_STEP * V_HEAD), 1)
    o0 = acc_ref[0] / l_ref[0]
    o1 = acc_ref[1] / l_ref[1]
    o_ref[0] = jnp.where(lane < V_HEAD, o0, o1)


def _attn(q, k, v, *, tq):
    b, s, _ = q.shape
    hw = HEADS_PER_STEP * HEAD_BLOCK
    vw = HEADS_PER_STEP * V_HEAD
    return pl.pallas_call(
        functools.partial(_attn_kernel, tq=tq),
        grid=(b, N_HEADS // HEADS_PER_STEP, s // tq),
        in_specs=[pl.BlockSpec((1, tq, hw), lambda bi, hp, i: (bi, i, hp)),
                  pl.BlockSpec((1, s, hw), lambda bi, hp, i: (bi, 0, hp)),
                  pl.BlockSpec((1, s, vw), lambda bi, hp, i: (bi, 0, hp))],
        out_specs=pl.BlockSpec((1, tq, vw), lambda bi, hp, i: (bi, i, hp)),
        out_shape=jax.ShapeDtypeStruct((b, s, N_HEADS * V_HEAD), F32),
        scratch_shapes=[pltpu.VMEM((HEADS_PER_STEP, tq, tq), F32), pltpu.VMEM((HEADS_PER_STEP, tq, tq), BF16),
                        pltpu.VMEM((HEADS_PER_STEP, tq, LANES), F32), pltpu.VMEM((HEADS_PER_STEP, tq, LANES), F32),
                        pltpu.VMEM((HEADS_PER_STEP, tq, vw), F32)],
        compiler_params=_params("parallel", "parallel", "arbitrary"),
        name="attn_prompt",
    )(q, k, v)


def _absorb_kernel(q_ref, w_ref, qlat_ref, qrope_ref, *, n_seq, ts):
    blk = q_ref[...]
    qlat = _dot(blk, w_ref[0])
    qlat_ref[...] = qlat.reshape(n_seq, ts, KV_LORA)
    rolled = pltpu.roll(blk.astype(F32), LANES - QK_NOPE, 1)
    qrope_ref[...] = rolled[:, 0:QK_ROPE].reshape(n_seq, ts, QK_ROPE)


def _absorb(q, w_ukt, *, n_seq, ts):
    m = q.shape[0]
    return pl.pallas_call(
        functools.partial(_absorb_kernel, n_seq=n_seq, ts=ts),
        grid=(N_HEADS,),
        in_specs=[pl.BlockSpec((m, HEAD_BLOCK), lambda h: (0, h)),
                  pl.BlockSpec((1, HEAD_BLOCK, KV_LORA), lambda h: (h, 0, 0))],
        out_specs=[pl.BlockSpec((n_seq, None, ts, KV_LORA), lambda h: (0, h, 0, 0)),
                   pl.BlockSpec((n_seq, None, ts, QK_ROPE), lambda h: (0, h, 0, 0))],
        out_shape=[jax.ShapeDtypeStruct((n_seq, N_HEADS, ts, KV_LORA), F32),
                   jax.ShapeDtypeStruct((n_seq, N_HEADS, ts, QK_ROPE), F32)],
        compiler_params=_params("parallel"),
        name="absorb",
    )(q, w_ukt)


def _paged_attn_kernel(pt_ref, qlat_ref, qrope_ref, new_ref, cache_ref, o_ref, buf_ref, kl_ref, kr_ref, sem,
                       *, n_pages, ts, layer):
    b = pl.program_id(0)
    nq = N_HEADS * ts
    slot = b % 2

    def page_copy(seq, j, slot_):
        page = pt_ref[seq * n_pages + j]
        return pltpu.make_async_copy(cache_ref.at[layer, page], buf_ref.at[slot_, j], sem.at[slot_])

    @pl.when(b == 0)
    def _():
        for j in range(n_pages):
            page_copy(0, j, 0).start()

    @pl.when(b + 1 < pl.num_programs(0))
    def _():
        for j in range(n_pages):
            page_copy(b + 1, j, 1 - slot).start()

    for j in range(n_pages):
        page_copy(b, j, slot).wait()
    for j in range(n_pages):
        pg = buf_ref[slot, j]
        kl_ref[j * PAGE_SIZE:(j + 1) * PAGE_SIZE, :] = pg[:, 0:KV_LORA].astype(BF16)
        kr_ref[j * PAGE_SIZE:(j + 1) * PAGE_SIZE, :] = pg[:, KV_LORA:LATENT].astype(BF16)
    ql = qlat_ref[0].reshape(nq, KV_LORA).astype(BF16)
    qr = qrope_ref[0].reshape(nq, QK_ROPE).astype(BF16)
    kl = kl_ref[...]
    s = _dot_nt(ql, kl) + _dot_nt(qr, kr_ref[...])
    new = new_ref[0]
    nl = new[:, 0:KV_LORA].astype(BF16)
    s_new = _dot_nt(ql, nl) + _dot_nt(qr, new[:, KV_LORA:LATENT].astype(BF16))
    tok = lax.broadcasted_iota(jnp.int32, (nq, ts), 0) % ts
    key = lax.broadcasted_iota(jnp.int32, (nq, ts), 1)
    s_new = jnp.where(key <= tok, s_new, NEG_INF)
    m = jnp.maximum(jnp.max(s, axis=-1, keepdims=True), jnp.max(s_new, axis=-1, keepdims=True))
    p = jnp.exp2(s - m)
    p_new = jnp.exp2(s_new - m)
    l = jnp.sum(p, axis=-1, keepdims=True) + jnp.sum(p_new, axis=-1, keepdims=True)
    o = (_dot(p.astype(BF16), kl) + _dot(p_new.astype(BF16), nl)) / l
    o_ref[0] = o.reshape(N_HEADS, ts, KV_LORA)


def _paged_attn(page_table, qlat, qrope, lat_new, cache, layer):
    n_seq, n_pages = page_table.shape
    ts = qlat.shape[2]

    grid_spec = pltpu.PrefetchScalarGridSpec(
        num_scalar_prefetch=1,
        grid=(n_seq,),
        in_specs=[pl.BlockSpec((1, N_HEADS, ts, KV_LORA), lambda b, pt: (b, 0, 0, 0)),
                  pl.BlockSpec((1, N_HEADS, ts, QK_ROPE), lambda b, pt: (b, 0, 0, 0)),
                  pl.BlockSpec((1, ts, LATENT), lambda b, pt: (b, 0, 0)),
                  pl.BlockSpec(memory_space=pltpu.HBM)],
        out_specs=pl.BlockSpec((1, N_HEADS, ts, KV_LORA), lambda b, pt: (b, 0, 0, 0)),
        scratch_shapes=[pltpu.VMEM((2, n_pages, PAGE_SIZE, LATENT), F32),
                        pltpu.VMEM((n_pages * PAGE_SIZE, KV_LORA), BF16),
                        pltpu.VMEM((n_pages * PAGE_SIZE, QK_ROPE), BF16),
                        pltpu.SemaphoreType.DMA((2,))],
    )
    return pl.pallas_call(
        functools.partial(_paged_attn_kernel, n_pages=n_pages, ts=ts, layer=layer),
        grid_spec=grid_spec,
        out_shape=jax.ShapeDtypeStruct((n_seq, N_HEADS, ts, KV_LORA), F32),
        compiler_params=_params("arbitrary"),
        name="paged_attn",
    )(page_table.reshape(-1), qlat, qrope, lat_new, cache)


def _v_up_kernel(o_ref, w_ref, out_ref, *, n_seq, ts):
    acc = None
    for e in range(HEADS_PER_STEP):
        x = o_ref[:, e].reshape(n_seq * ts, KV_LORA).astype(BF16)
        d = _dot(x, w_ref[e])
        acc = d if acc is None else acc + d
    out_ref[...] = acc


def _v_up(olat, w_vup):
    n_seq, _, ts, _ = olat.shape
    vw = HEADS_PER_STEP * V_HEAD
    return pl.pallas_call(
        functools.partial(_v_up_kernel, n_seq=n_seq, ts=ts),
        grid=(N_HEADS // HEADS_PER_STEP,),
        in_specs=[pl.BlockSpec((n_seq, HEADS_PER_STEP, ts, KV_LORA), lambda p: (0, p, 0, 0)),
                  pl.BlockSpec((HEADS_PER_STEP, KV_LORA, vw), lambda p: (p, 0, 0))],
        out_specs=pl.BlockSpec((n_seq * ts, vw), lambda p: (0, p)),
        out_shape=jax.ShapeDtypeStruct((n_seq * ts, N_HEADS * V_HEAD), F32),
        compiler_params=_params("parallel"),
        name="v_up",
    )(olat, w_vup)


def _mix_kernel(x_ref, u_ref, prev_ref, gate_ref, o_ref, g1_ref, wpool_ref, pscale_ref, wout_ref,
                x1_ref, ext_ref, *, bb, ts, pos0, prompt):
    j = pl.program_id(1)
    tm = bb * ts
    u = u_ref[...]
    ext_ref[:, POOL_HALO:, :] = u
    if prompt:
        ext_ref[:, 0:POOL_HALO, :] = jnp.where(j > 0, prev_ref[...], 0.0)
    else:
        ext_ref[:, POOL_HALO - POOL_CTX:POOL_HALO, :] = prev_ref[...]
    pos = pos0 + j * ts + lax.broadcasted_iota(jnp.int32, (1, ts, 1), 1)
    mixed = []
    for g, w in enumerate(POOL_WINDOWS):
        cs = slice(g * POOL_GW, (g + 1) * POOL_GW)
        acc = ext_ref[:, POOL_HALO:POOL_HALO + ts, cs]
        for t in range(1, w):
            acc = acc + ext_ref[:, POOL_HALO - t:POOL_HALO - t + ts, cs]
        cnt = jnp.minimum(pos + 1, w).astype(F32)
        y = acc / cnt - u[:, :, cs]
        z = _dot(y.reshape(tm, POOL_GW).astype(BF16), wpool_ref[g]) * pscale_ref[:, cs]
        ga = gate_ref[:, :, cs].reshape(tm, POOL_GW)
        gb = gate_ref[:, :, D_MODEL + g * POOL_GW:D_MODEL + (g + 1) * POOL_GW].reshape(tm, POOL_GW)
        mixed.append((ga * z + gb * o_ref[:, :, cs].reshape(tm, POOL_GW)).astype(BF16))
    out = _dot(jnp.concatenate(mixed, axis=-1), wout_ref[...])
    x1 = x_ref[...].reshape(tm, D_MODEL) + _rows(g1_ref, bb, ts) * out
    x1_ref[...] = x1.reshape(bb, ts, D_MODEL)


def _mix(x, u, prev, gate, o, mod, lw, *, bb, ts, pos0, prompt):
    b, s, _ = x.shape
    tok = lambda n: pl.BlockSpec((bb, ts, n), lambda bi, j: (bi, j, 0))
    if prompt:
        r = ts // POOL_HALO
        prev_spec = pl.BlockSpec((bb, POOL_HALO, D_MODEL), lambda bi, j: (bi, jnp.maximum(j * r - 1, 0), 0))
    else:
        prev_spec = pl.BlockSpec((bb, POOL_CTX, D_MODEL), lambda bi, j: (bi, 0, 0))
    return pl.pallas_call(
        functools.partial(_mix_kernel, bb=bb, ts=ts, pos0=pos0, prompt=prompt),
        grid=(b // bb, s // ts),
        in_specs=[tok(D_MODEL), tok(D_MODEL), prev_spec, tok(2 * D_MODEL), tok(D_MODEL),
                  pl.BlockSpec((bb, 1, D_MODEL), lambda bi, j: (bi, 0, 2)),
                  _const_spec((len(POOL_WINDOWS), POOL_GW, POOL_GW)), _const_spec((1, D_MODEL)),
                  _const_spec((D_MODEL, D_MODEL))],
        out_specs=tok(D_MODEL),
        out_shape=jax.ShapeDtypeStruct((b, s, D_MODEL), F32),
        scratch_shapes=[pltpu.VMEM((bb, POOL_HALO + ts, D_MODEL), F32)],
        compiler_params=_params("parallel", "arbitrary"),
        name="mix_prompt" if prompt else "mix_sample",
    )(x, u, prev, gate, o, mod, lw["w_pool"], lw["pool_scale"], lw["w_out"])


FF_CHUNK = D_FF // 2
CONV_HALO = SUBLANES


def _ffn_kernel(x_ref, prev_ref, sh_ref, sc_ref, g2_ref, gf_ref, wup_ref, wconv_ref, bconv_ref, wdown_ref,
                gfin_ref, y_ref, cnew_ref, extg_ref, extv_ref, *, bb, ts, prompt, final):
    j = pl.program_id(1)
    tm = bb * ts
    x = x_ref[...].reshape(tm, D_MODEL)
    sc = _rows(sc_ref, bb, ts)
    sh = _rows(sh_ref, bb, ts)
    if prompt:
        xe = jnp.concatenate([prev_ref[0], x], axis=0)
    else:
        xe = x
    hb = (_rms(xe, gf_ref[...]) * (1.0 + sc) + sh).astype(BF16)
    acc = jnp.zeros((tm, D_MODEL), F32)
    for c in range(D_FF // FF_CHUNK):
        conv = []
        for part, ext_ref in ((0, extg_ref), (1, extv_ref)):
            lo = part * D_FF + c * FF_CHUNK
            cs = slice(lo, lo + FF_CHUNK)
            z = _dot(hb, wup_ref[:, cs])
            if prompt:
                keep = jnp.logical_or(j > 0, lax.broadcasted_iota(jnp.int32, (tm + CONV_HALO, 1), 0) >= CONV_HALO)
                ext_ref[0] = jnp.where(keep, z, 0.0)
            else:
                ext_ref[:, CONV_HALO:, :] = z.reshape(bb, ts, FF_CHUNK)
                ext_ref[:, CONV_HALO - (CONV_W - 1):CONV_HALO, :] = prev_ref[:, :, cs]
            out = bconv_ref[:, cs][None]
            for t in range(CONV_W):
                lo_r = CONV_HALO - (CONV_W - 1) + t
                out = out + ext_ref[:, lo_r:lo_r + ts, :] * wconv_ref[t:t + 1, cs][None]
            cnew_ref[:, :, cs] = ext_ref[:, CONV_HALO + ts - (CONV_W - 1):CONV_HALO + ts, :]
            conv.append(out.reshape(tm, FF_CHUNK))
        act = (jax.nn.silu(conv[0]) * conv[1]).astype(BF16)
        acc = acc + _dot(act, wdown_ref[c * FF_CHUNK:(c + 1) * FF_CHUNK, :])
    x2 = x + _rows(g2_ref, bb, ts) * acc
    if final:
        x2 = _rms(x2, gfin_ref[...])
    y_ref[...] = x2.reshape(bb, ts, D_MODEL)


def _ffn(x, prev, mod, lw, g_final, *, bb, ts, prompt, final):
    b, s, _ = x.shape
    tok = pl.BlockSpec((bb, ts, D_MODEL), lambda bi, j: (bi, j, 0))
    if prompt:
        r = ts // CONV_HALO
        prev_spec = pl.BlockSpec((bb, CONV_HALO, D_MODEL), lambda bi, j: (bi, jnp.maximum(j * r - 1, 0), 0))
        halo = CONV_HALO
    else:
        prev_spec = pl.BlockSpec((bb, CONV_W - 1, 2 * D_FF), lambda bi, j: (bi, 0, 0))
        halo = 0
    modspec = lambda k: pl.BlockSpec((bb, 1, D_MODEL), lambda bi, j: (bi, 0, k))
    return pl.pallas_call(
        functools.partial(_ffn_kernel, bb=bb, ts=ts, prompt=prompt, final=final),
        grid=(b // bb, s // ts),
        in_specs=[tok, prev_spec, modspec(3), modspec(4), modspec(5), _const_spec((1, D_MODEL)),
                  _const_spec((D_MODEL, 2 * D_FF)), _const_spec((CONV_W, 2 * D_FF)), _const_spec((1, 2 * D_FF)),
                  _const_spec((D_FF, D_MODEL)), _const_spec((1, D_MODEL))],
        out_specs=[tok, pl.BlockSpec((bb, CONV_W - 1, 2 * D_FF), lambda bi, j: (bi, 0, 0))],
        out_shape=[jax.ShapeDtypeStruct((b, s, D_MODEL), F32),
                   jax.ShapeDtypeStruct((b, CONV_W - 1, 2 * D_FF), F32)],
        scratch_shapes=[pltpu.VMEM((bb, CONV_HALO + ts, FF_CHUNK), F32),
                        pltpu.VMEM((bb, CONV_HALO + ts, FF_CHUNK), F32)],
        compiler_params=_params("parallel", "arbitrary"),
        name=("ffn_prompt" if prompt else "ffn_sample") + ("_final" if final else ""),
    )(x, prev, mod, mod, mod, lw["g_ffn"], lw["w_up"], lw["w_conv"], lw["b_conv"], lw["w_down"], g_final)


def _rope_tables(pos):
    freq = ROPE_THETA ** (-jnp.arange(ROPE_HALF, dtype=F32) / ROPE_HALF)
    ang = pos[:, None] * freq[None, :]
    cos, sin = jnp.cos(ang), jnp.sin(ang)
    z16 = jnp.zeros_like(cos)
    one = jnp.ones((pos.shape[0], QK_NOPE), F32)
    zero = jnp.zeros((pos.shape[0], QK_NOPE), F32)
    pad = jnp.zeros((pos.shape[0], LANES - QK_NOPE - QK_ROPE), F32)
    span0, span1, span2 = [cos, cos], [z16, sin], [-sin, z16]
    cat = lambda parts: jnp.concatenate(parts, axis=-1)
    tab_k = jnp.stack([cat(span0 + [pad] + span0 + [pad]), cat(span1 + [pad] + span1 + [pad]),
                       cat(span2 + [pad] + span2 + [pad])])
    tab_q = jnp.stack([cat([one] + span0 + [pad]), cat([zero] + span1 + [pad]), cat([zero] + span2 + [pad])])
    return tab_k, tab_q * Q_PRESCALE


def _prep_layer(l, w_in, g_attn, g_q, w_uq, g_kv, w_uk, w_uv, w_pool, pool_scale, w_out, g_ffn, w_up,
                w_conv, b_conv, w_down):
    wi = w_in[l]
    o2 = IN_A
    o3 = o2 + QK_ROPE
    kr = wi[:, o2:o3]
    zpad = jnp.zeros((D_MODEL, LANES // 2 - QK_ROPE), F32)
    w_in_p = jnp.concatenate([wi[:, :o2], kr, zpad, kr, zpad, wi[:, o3:]], axis=1).astype(BF16)
    hpad = HEAD_BLOCK - QK_NOPE - QK_ROPE
    w_q = jnp.pad(w_uq[l].reshape(Q_LORA, N_HEADS, QK_NOPE + QK_ROPE), ((0, 0), (0, 0), (0, hpad)))
    w_k = jnp.pad(w_uk[l], ((0, 0), (0, 0), (0, HEAD_BLOCK - QK_NOPE)))
    w_ukt = jnp.pad(jnp.transpose(w_uk[l], (1, 2, 0)), ((0, 0), (0, HEAD_BLOCK - QK_NOPE), (0, 0)))
    wv_heads = jnp.transpose(w_uv[l], (1, 0, 2))
    w_vup = jnp.stack([jnp.pad(wv_heads[h], ((0, 0), ((h % HEADS_PER_STEP) * V_HEAD,
                                                       (HEADS_PER_STEP - 1 - h % HEADS_PER_STEP) * V_HEAD)))
                       for h in range(N_HEADS)])
    return dict(
        w_in=w_in_p, g_attn=g_attn[l][None], g_q=g_q[l][None], g_kv=g_kv[l][None],
        w_q=w_q.reshape(Q_LORA, N_HEADS * HEAD_BLOCK).astype(BF16),
        w_k=w_k.reshape(KV_LORA, N_HEADS * HEAD_BLOCK).astype(BF16),
        w_v=w_uv[l].reshape(KV_LORA, N_HEADS * V_HEAD).astype(BF16),
        w_ukt=w_ukt.astype(BF16), w_vup=w_vup.astype(BF16),
        w_pool=w_pool[l].astype(BF16), pool_scale=pool_scale[l][None], w_out=w_out[l].astype(BF16),
        g_ffn=g_ffn[l][None], w_up=w_up[l].astype(BF16), w_conv=w_conv[l], b_conv=b_conv[l][None],
        w_down=w_down[l].astype(BF16))


def _pick(n, pref):
    t = min(n, pref)
    while n % t:
        t -= 1
    return t


def kernel(x_prompt, x_sample, cache_latent, state_pool, state_conv, page_table, c_prompt, c_sample, w_ada, b_ada, g_attn, w_in, g_q, w_uq, g_kv, w_uk, w_uv, w_pool, pool_scale, w_out, g_ffn, w_up, w_conv, b_conv, w_down, g_final):
    depth = w_ada.shape[0]
    bp, sp, _ = x_prompt.shape
    bs, ss, _ = x_sample.shape
    past_len = page_table.shape[1] * PAGE_SIZE

    mod_all = _ada(jnp.concatenate([c_prompt, c_sample], axis=0), w_ada.astype(BF16), b_ada[:, None, :])
    tabk_p, tabq_p = _rope_tables(jnp.arange(sp, dtype=F32))
    tabk_s, tabq_s = _rope_tables(past_len + jnp.arange(ss, dtype=F32))
    gfin = g_final[None]

    ts_p = _pick(sp, 512)
    ts_ffn = _pick(sp, 256)
    bb_s = _pick(bs, 32)

    yp, ys = x_prompt, x_sample
    outs = [[] for _ in range(6)]
    for l in range(depth):
        lw = _prep_layer(l, w_in, g_attn, g_q, w_uq, g_kv, w_uk, w_uv, w_pool, pool_scale, w_out, g_ffn,
                         w_up, w_conv, b_conv, w_down)
        final = l == depth - 1
        mod_p = mod_all[l, :bp][:, None, :]
        mod_s = mod_all[l, bp:][:, None, :]

        cqn, lat, u, gate, kf, vf = _in_proj(yp.reshape(bp * sp, D_MODEL), mod_p, tabk_p, lw,
                                             n_batch=bp, seq=sp, bb=1, ts=ts_p, prompt=True)
        q = _q_proj(cqn, tabq_p, lw["w_q"], seq=sp, bb=1, ts=ts_p, prompt=True)
        o = _attn(q.reshape(bp, sp, -1), kf.reshape(bp, sp, -1), vf.reshape(bp, sp, -1), tq=ts_p)
        u3 = u.reshape(bp, sp, D_MODEL)
        x1 = _mix(yp, u3, u3, gate.reshape(bp, sp, -1), o, mod_p, lw, bb=1, ts=ts_p, pos0=0, prompt=True)
        yp, conv_p = _ffn(x1, x1, mod_p, lw, gfin, bb=1, ts=ts_ffn, prompt=True, final=final)
        outs[0].append(lat.reshape(bp, sp, LATENT))
        outs[1].append(u3[:, sp - POOL_CTX:, :])
        outs[2].append(conv_p)

        cqn, lat, u, gate = _in_proj(ys.reshape(bs * ss, D_MODEL), mod_s, tabk_s, lw,
                                     n_batch=bs, seq=ss, bb=bb_s, ts=ss, prompt=False)
        q = _q_proj(cqn, tabq_s, lw["w_q"], seq=ss, bb=bb_s, ts=ss, prompt=False)
        qlat, qrope = _absorb(q, lw["w_ukt"], n_seq=bs, ts=ss)
        lat3 = lat.reshape(bs, ss, LATENT)
        olat = _paged_attn(page_table, qlat, qrope, lat3, cache_latent, l)
        o = _v_up(olat, lw["w_vup"])
        u3 = u.reshape(bs, ss, D_MODEL)
        x1 = _mix(ys, u3, state_pool[l], gate.reshape(bs, ss, -1), o.reshape(bs, ss, -1), mod_s, lw,
                  bb=bb_s, ts=ss, pos0=past_len, prompt=False)
        ys, conv_s = _ffn(x1, state_conv[l], mod_s, lw, gfin, bb=bb_s, ts=ss, prompt=False, final=final)
        outs[3].append(lat3)
        outs[4].append(jnp.concatenate([state_pool[l], u3], axis=1)[:, -POOL_CTX:, :])
        outs[5].append(conv_s)

    return (yp, ys) + tuple(jnp.stack(o) for o in outs)
```

```python
import functools

import jax
import jax.numpy as jnp
from jax import lax
from jax.experimental import pallas as pl
from jax.experimental.pallas import tpu as pltpu

F32 = jnp.float32
BF16 = jnp.bfloat16

D_MODEL = 1024
N_HEADS = 16
QK_NOPE = 64
QK_ROPE = 32
ROPE_HALF = QK_ROPE // 2
V_HEAD = 64
KV_LORA = 256
Q_LORA = 768
LATENT = KV_LORA + QK_ROPE
ROPE_THETA = 10000.0
ATTN_SCALE = (QK_NOPE + QK_ROPE) ** -0.5
Q_PRESCALE = ATTN_SCALE * 1.4426950408889634
POOL_WINDOWS = (2, 4, 8, 16)
POOL_GW = D_MODEL // len(POOL_WINDOWS)
POOL_CTX = 15
D_FF = 2816
CONV_W = 3
PAGE_SIZE = 128
EPS = 1e-6
NEG_INF = -1e30

LANES = 128
SUBLANES = 8
HEAD_BLOCK = LANES
POOL_HALO = 16
IN_A = Q_LORA + KV_LORA
IN_KR = IN_A + LANES
IN_U = IN_KR + D_MODEL
IN_END = IN_U + 2 * D_MODEL
VMEM_LIMIT = 56 * 1024 * 1024

NT_DIMS = (((1,), (1,)), ((), ()))


def _dot(a, b):
    return jnp.dot(a, b, preferred_element_type=F32)


def _dot_nt(a, b):
    return lax.dot_general(a, b, NT_DIMS, preferred_element_type=F32)


def _rms(x, g):
    return x * lax.rsqrt(jnp.mean(x * x, axis=-1, keepdims=True) + EPS) * g


def _rows(ref, bb, ts):
    v = ref[...]
    n = v.shape[-1]
    if bb == 1:
        return v[0]
    return jnp.broadcast_to(v, (bb, ts, n)).reshape(bb * ts, n)


def _table_rows(tab_ref, idx, bb, ts):
    t = tab_ref[idx]
    if bb == 1:
        return t
    return jnp.broadcast_to(t[None], (bb, ts, LANES)).reshape(bb * ts, LANES)


def _rope_block(blk, t0, t1, t2):
    return blk * t0 + pltpu.roll(blk, ROPE_HALF, 1) * t1 + pltpu.roll(blk, LANES - ROPE_HALF, 1) * t2


def _const_spec(shape):
    nd = len(shape)
    return pl.BlockSpec(shape, lambda *_: (0,) * nd, pipeline_mode=pl.Buffered(1))


def _params(*sem):
    return pltpu.CompilerParams(dimension_semantics=sem, vmem_limit_bytes=VMEM_LIMIT)


def _ada_kernel(c_ref, w_ref, b_ref, o_ref):
    o_ref[...] = _dot(c_ref[...].astype(BF16), w_ref[...]) + b_ref[...]


def _ada(c_all, w_ada, b_ada):
    depth, d, n = w_ada.shape
    rows = c_all.shape[0]
    tn = D_MODEL
    return pl.pallas_call(
        _ada_kernel,
        grid=(depth, n // tn),
        in_specs=[pl.BlockSpec((rows, d), lambda l, j: (0, 0)),
                  pl.BlockSpec((None, d, tn), lambda l, j: (l, 0, j)),
                  pl.BlockSpec((None, 1, tn), lambda l, j: (l, 0, j))],
        out_specs=pl.BlockSpec((None, rows, tn), lambda l, j: (l, 0, j)),
        out_shape=jax.ShapeDtypeStruct((depth, rows, n), F32),
        compiler_params=_params("parallel", "parallel"),
        name="ada",
    )(c_all, w_ada, b_ada)


def _in_proj_kernel(*refs, bb, ts, prompt):
    if prompt:
        (x_ref, sh_ref, sc_ref, tab_ref, ga_ref, gq_ref, gkv_ref, win_ref, wk_ref, wv_ref,
         cqn_ref, lat_ref, u_ref, gate_ref, k_ref, v_ref) = refs
    else:
        (x_ref, sh_ref, sc_ref, tab_ref, ga_ref, gq_ref, gkv_ref, win_ref,
         cqn_ref, lat_ref, u_ref, gate_ref) = refs
    x = x_ref[...]
    h = _rms(x, ga_ref[...]) * (1.0 + _rows(sc_ref, bb, ts)) + _rows(sh_ref, bb, ts)
    hb = h.astype(BF16)
    a = _dot(hb, win_ref[:, 0:IN_A])
    cqn_ref[...] = _rms(a[:, 0:Q_LORA], gq_ref[...]).astype(BF16)
    lat = _rms(a[:, Q_LORA:IN_A], gkv_ref[...])
    lat_ref[:, 0:KV_LORA] = lat
    kr = _dot(hb, win_ref[:, IN_A:IN_KR])
    krr = _rope_block(kr, _table_rows(tab_ref, 0, bb, ts), _table_rows(tab_ref, 1, bb, ts),
                      _table_rows(tab_ref, 2, bb, ts))
    lat_ref[:, KV_LORA:LATENT] = krr[:, 0:QK_ROPE]
    u_ref[...] = _dot(hb, win_ref[:, IN_KR:IN_U])
    gate_ref[...] = jax.nn.sigmoid(_dot(hb, win_ref[:, IN_U:IN_END]))
    if prompt:
        latb = lat.astype(BF16)
        lane = lax.broadcasted_iota(jnp.int32, krr.shape, 1)
        kr_head = jnp.where(lane >= QK_NOPE, krr, 0.0)
        kf = _dot(latb, wk_ref[...])
        vf = _dot(latb, wv_ref[...])
        for hd in range(N_HEADS):
            sl = slice(hd * HEAD_BLOCK, (hd + 1) * HEAD_BLOCK)
            k_ref[:, sl] = (kf[:, sl] + kr_head).astype(BF16)
            v_ref[:, sl] = jnp.where(lane == V_HEAD, 1.0, vf[:, sl]).astype(BF16)


def _in_proj(x, mod, tab, lw, *, n_batch, seq, bb, ts, prompt):
    m = x.shape[0]
    tm = bb * ts
    nst = seq // ts
    if prompt:
        row = lambda i: (i, 0)
        modmap = lambda k: (lambda i: (i // nst, 0, k))
        tabmap = lambda i: (0, i % nst, 0)
    else:
        row = lambda i: (i, 0)
        modmap = lambda k: (lambda i: (i, 0, k))
        tabmap = lambda i: (0, 0, 0)
    in_specs = [pl.BlockSpec((tm, D_MODEL), row),
                pl.BlockSpec((bb, 1, D_MODEL), modmap(0)),
                pl.BlockSpec((bb, 1, D_MODEL), modmap(1)),
                pl.BlockSpec((3, ts, LANES), tabmap),
                _const_spec((1, D_MODEL)), _const_spec((1, Q_LORA)), _const_spec((1, KV_LORA)),
                _const_spec((D_MODEL, IN_END))]
    args = [x, mod, mod, tab, lw["g_attn"], lw["g_q"], lw["g_kv"], lw["w_in"]]
    out_specs = [pl.BlockSpec((tm, Q_LORA), row), pl.BlockSpec((tm, LATENT), row),
                 pl.BlockSpec((tm, D_MODEL), row), pl.BlockSpec((tm, 2 * D_MODEL), row)]
    out_shape = [jax.ShapeDtypeStruct((m, Q_LORA), BF16), jax.ShapeDtypeStruct((m, LATENT), F32),
                 jax.ShapeDtypeStruct((m, D_MODEL), F32), jax.ShapeDtypeStruct((m, 2 * D_MODEL), F32)]
    if prompt:
        in_specs += [_const_spec((KV_LORA, N_HEADS * HEAD_BLOCK))] * 2
        args += [lw["w_k"], lw["w_v"]]
        out_specs += [pl.BlockSpec((tm, N_HEADS * HEAD_BLOCK), row)] * 2
        out_shape += [jax.ShapeDtypeStruct((m, N_HEADS * HEAD_BLOCK), BF16)] * 2
    return pl.pallas_call(
        functools.partial(_in_proj_kernel, bb=bb, ts=ts, prompt=prompt),
        grid=(m // tm,), in_specs=in_specs, out_specs=out_specs, out_shape=out_shape,
        compiler_params=_params("parallel"),
        name="in_proj_prompt" if prompt else "in_proj_sample",
    )(*args)


def _q_proj_kernel(cqn_ref, tab_ref, wq_ref, q_ref, *, bb, ts):
    q = _dot(cqn_ref[...], wq_ref[...])
    t0, t1, t2 = (_table_rows(tab_ref, i, bb, ts) for i in range(3))
    for hd in range(N_HEADS):
        sl = slice(hd * HEAD_BLOCK, (hd + 1) * HEAD_BLOCK)
        q_ref[:, sl] = _rope_block(q[:, sl], t0, t1, t2).astype(BF16)


def _q_proj(cqn, tab, w_q, *, seq, bb, ts, prompt):
    m = cqn.shape[0]
    tm = bb * ts
    nst = seq // ts
    tabmap = (lambda i: (0, i % nst, 0)) if prompt else (lambda i: (0, 0, 0))
    return pl.pallas_call(
        functools.partial(_q_proj_kernel, bb=bb, ts=ts),
        grid=(m // tm,),
        in_specs=[pl.BlockSpec((tm, Q_LORA), lambda i: (i, 0)),
                  pl.BlockSpec((3, ts, LANES), tabmap),
                  _const_spec((Q_LORA, N_HEADS * HEAD_BLOCK))],
        out_specs=pl.BlockSpec((tm, N_HEADS * HEAD_BLOCK), lambda i: (i, 0)),
        out_shape=jax.ShapeDtypeStruct((m, N_HEADS * HEAD_BLOCK), BF16),
        compiler_params=_params("parallel"),
        name="q_proj_prompt" if prompt else "q_proj_sample",
    )(cqn, tab, w_q)


HEADS_PER_STEP = 2


ATTN_STRIP = 16


def _attn_kernel(q_ref, k_ref, v_ref, o_ref, s_ref, p_ref, m_ref, acc_ref, *, tq):
    i = pl.program_id(2)
    reps = tq // LANES
    m_ref[...] = jnp.full(m_ref.shape, -jnp.inf, F32)
    acc_ref[...] = jnp.zeros(acc_ref.shape, F32)

    def tile(j, masked):
        start = pl.multiple_of(j * tq, tq)
        for hh in range(HEADS_PER_STEP):
            hs = slice(hh * HEAD_BLOCK, (hh + 1) * HEAD_BLOCK)
            s_ref[hh] = _dot_nt(q_ref[0, :, hs], k_ref[0, pl.ds(start, tq), hs])
        for hh in range(HEADS_PER_STEP):
            hs = slice(hh * HEAD_BLOCK, (hh + 1) * HEAD_BLOCK)
            for r in range(tq // ATTN_STRIP):
                rows = slice(r * ATTN_STRIP, (r + 1) * ATTN_STRIP)
                s = s_ref[hh, rows, :]
                if masked:
                    qi = r * ATTN_STRIP + lax.broadcasted_iota(jnp.int32, (ATTN_STRIP, tq), 0)
                    ki = lax.broadcasted_iota(jnp.int32, (ATTN_STRIP, tq), 1)
                    s = jnp.where(ki <= qi, s, NEG_INF)
                m_prev = m_ref[hh, rows, :]
                m_new = jnp.maximum(m_prev, jnp.max(s, axis=-1, keepdims=True))
                alpha = jnp.exp2(m_prev - m_new)
                p = jnp.exp2(s - jnp.tile(m_new, (1, reps)))
                m_ref[hh, rows, :] = m_new
                acc_ref[hh, rows, :] = alpha * acc_ref[hh, rows, :]
                p_ref[hh, rows, :] = p.astype(BF16)
            acc_ref[hh] += _dot(p_ref[hh], v_ref[0, pl.ds(start, tq), hs])

    def unmasked(j, c):
        tile(j, False)
        return c

    lax.fori_loop(0, i, unmasked, 0)
    tile(i, True)
    outs = []
    for hh in range(HEADS_PER_STEP):
        acc = acc_ref[hh]
        outs.append(acc / acc[:, V_HEAD:V_HEAD + 1])
    lane = lax.broadcasted_iota(jnp.int32, (tq, HEAD_BLOCK), 1)
    o_ref[0] = jnp.where(lane < V_HEAD, outs[0], pltpu.roll(outs[1], V_HEAD, 1))


def _attn(q, k, v, *, tq):
    b, s, _ = q.shape
    hw = HEADS_PER_STEP * HEAD_BLOCK
    vw = HEADS_PER_STEP * V_HEAD
    return pl.pallas_call(
        functools.partial(_attn_kernel, tq=tq),
        grid=(b, N_HEADS // HEADS_PER_STEP, s // tq),
        in_specs=[pl.BlockSpec((1, tq, hw), lambda bi, hp, i: (bi, i, hp)),
                  pl.BlockSpec((1, s, hw), lambda bi, hp, i: (bi, 0, hp)),
                  pl.BlockSpec((1, s, hw), lambda bi, hp, i: (bi, 0, hp))],
        out_specs=pl.BlockSpec((1, tq, vw), lambda bi, hp, i: (bi, i, hp)),
        out_shape=jax.ShapeDtypeStruct((b, s, N_HEADS * V_HEAD), F32),
        scratch_shapes=[pltpu.VMEM((HEADS_PER_STEP, tq, tq), F32), pltpu.VMEM((HEADS_PER_STEP, tq, tq), BF16),
                        pltpu.VMEM((HEADS_PER_STEP, tq, LANES), F32),
                        pltpu.VMEM((HEADS_PER_STEP, tq, HEAD_BLOCK), F32)],
        compiler_params=_params("parallel", "parallel", "arbitrary"),
        name="attn_prompt",
    )(q, k, v)


def _absorb_kernel(q_ref, w_ref, qlat_ref, qrope_ref, *, n_seq, ts):
    blk = q_ref[...]
    qlat = _dot(blk, w_ref[0])
    qlat_ref[...] = qlat.reshape(n_seq, ts, KV_LORA)
    rolled = pltpu.roll(blk.astype(F32), LANES - QK_NOPE, 1)
    qrope_ref[...] = rolled[:, 0:QK_ROPE].reshape(n_seq, ts, QK_ROPE)


def _absorb(q, w_ukt, *, n_seq, ts):
    m = q.shape[0]
    return pl.pallas_call(
        functools.partial(_absorb_kernel, n_seq=n_seq, ts=ts),
        grid=(N_HEADS,),
        in_specs=[pl.BlockSpec((m, HEAD_BLOCK), lambda h: (0, h)),
                  pl.BlockSpec((1, HEAD_BLOCK, KV_LORA), lambda h: (h, 0, 0))],
        out_specs=[pl.BlockSpec((n_seq, None, ts, KV_LORA), lambda h: (0, h, 0, 0)),
                   pl.BlockSpec((n_seq, None, ts, QK_ROPE), lambda h: (0, h, 0, 0))],
        out_shape=[jax.ShapeDtypeStruct((n_seq, N_HEADS, ts, KV_LORA), F32),
                   jax.ShapeDtypeStruct((n_seq, N_HEADS, ts, QK_ROPE), F32)],
        compiler_params=_params("parallel"),
        name="absorb",
    )(q, w_ukt)


def _paged_attn_kernel(pt_ref, qlat_ref, qrope_ref, new_ref, cache_ref, o_ref, buf_ref, kt_ref, sem,
                       *, n_pages, ts, layer):
    b = pl.program_id(0)
    nq = N_HEADS * ts
    slot = b % 2

    def page_copy(seq, j, slot_):
        page = pt_ref[seq * n_pages + j]
        return pltpu.make_async_copy(cache_ref.at[layer, page], buf_ref.at[slot_, j], sem.at[slot_])

    @pl.when(b == 0)
    def _():
        for j in range(n_pages):
            page_copy(0, j, 0).start()

    @pl.when(b + 1 < pl.num_programs(0))
    def _():
        for j in range(n_pages):
            page_copy(b + 1, j, 1 - slot).start()

    for j in range(n_pages):
        page_copy(b, j, slot).wait()
    for j in range(n_pages):
        kt_ref[:, j * PAGE_SIZE:(j + 1) * PAGE_SIZE] = buf_ref[slot, j].astype(BF16)
    ql = qlat_ref[0].reshape(nq, KV_LORA).astype(BF16)
    qr = qrope_ref[0].reshape(nq, QK_ROPE).astype(BF16)
    kt_lat = kt_ref[0:KV_LORA, :]
    s = _dot(ql, kt_lat) + _dot(qr, kt_ref[KV_LORA:LATENT, :])
    new = new_ref[0]
    nl = new[:, 0:KV_LORA].astype(BF16)
    s_new = _dot_nt(ql, nl) + _dot_nt(qr, new[:, KV_LORA:LATENT].astype(BF16))
    tok = lax.broadcasted_iota(jnp.int32, (nq, ts), 0) % ts
    key = lax.broadcasted_iota(jnp.int32, (nq, ts), 1)
    s_new = jnp.where(key <= tok, s_new, NEG_INF)
    m = jnp.maximum(jnp.max(s, axis=-1, keepdims=True), jnp.max(s_new, axis=-1, keepdims=True))
    p = jnp.exp2(s - m)
    p_new = jnp.exp2(s_new - m)
    l = jnp.sum(p, axis=-1, keepdims=True) + jnp.sum(p_new, axis=-1, keepdims=True)
    o = (_dot_nt(p.astype(BF16), kt_lat) + _dot(p_new.astype(BF16), nl)) / l
    o_ref[0] = o.reshape(N_HEADS, ts, KV_LORA)


def _paged_attn(page_table, qlat, qrope, lat_new, cache_t, layer):
    n_seq, n_pages = page_table.shape
    ts = qlat.shape[2]

    grid_spec = pltpu.PrefetchScalarGridSpec(
        num_scalar_prefetch=1,
        grid=(n_seq,),
        in_specs=[pl.BlockSpec((1, N_HEADS, ts, KV_LORA), lambda b, pt: (b, 0, 0, 0)),
                  pl.BlockSpec((1, N_HEADS, ts, QK_ROPE), lambda b, pt: (b, 0, 0, 0)),
                  pl.BlockSpec((1, ts, LATENT), lambda b, pt: (b, 0, 0)),
                  pl.BlockSpec(memory_space=pltpu.HBM)],
        out_specs=pl.BlockSpec((1, N_HEADS, ts, KV_LORA), lambda b, pt: (b, 0, 0, 0)),
        scratch_shapes=[pltpu.VMEM((2, n_pages, LATENT, PAGE_SIZE), F32),
                        pltpu.VMEM((LATENT, n_pages * PAGE_SIZE), BF16),
                        pltpu.SemaphoreType.DMA((2,))],
    )
    return pl.pallas_call(
        functools.partial(_paged_attn_kernel, n_pages=n_pages, ts=ts, layer=layer),
        grid_spec=grid_spec,
        out_shape=jax.ShapeDtypeStruct((n_seq, N_HEADS, ts, KV_LORA), F32),
        compiler_params=_params("arbitrary"),
        name="paged_attn",
    )(page_table.reshape(-1), qlat, qrope, lat_new, cache_t)


def _v_up_kernel(o_ref, w_ref, out_ref, *, n_seq, ts):
    acc = None
    for e in range(HEADS_PER_STEP):
        x = o_ref[:, e].reshape(n_seq * ts, KV_LORA).astype(BF16)
        d = _dot(x, w_ref[e])
        acc = d if acc is None else acc + d
    out_ref[...] = acc


def _v_up(olat, w_vup):
    n_seq, _, ts, _ = olat.shape
    vw = HEADS_PER_STEP * V_HEAD
    return pl.pallas_call(
        functools.partial(_v_up_kernel, n_seq=n_seq, ts=ts),
        grid=(N_HEADS // HEADS_PER_STEP,),
        in_specs=[pl.BlockSpec((n_seq, HEADS_PER_STEP, ts, KV_LORA), lambda p: (0, p, 0, 0)),
                  pl.BlockSpec((HEADS_PER_STEP, KV_LORA, vw), lambda p: (p, 0, 0))],
        out_specs=pl.BlockSpec((n_seq * ts, vw), lambda p: (0, p)),
        out_shape=jax.ShapeDtypeStruct((n_seq * ts, N_HEADS * V_HEAD), F32),
        compiler_params=_params("parallel"),
        name="v_up",
    )(olat, w_vup)


def _mix_kernel(x_ref, u_ref, prev_ref, gate_ref, o_ref, g1_ref, wpool_ref, pscale_ref, wout_ref,
                x1_ref, ext_ref, *, bb, ts, pos0, prompt):
    j = pl.program_id(1)
    tm = bb * ts
    u = u_ref[...]
    ext_ref[:, POOL_HALO:, :] = u
    if prompt:
        ext_ref[:, 0:POOL_HALO, :] = jnp.where(j > 0, prev_ref[...], 0.0)
    else:
        ext_ref[:, POOL_HALO - POOL_CTX:POOL_HALO, :] = prev_ref[...]
    pos = pos0 + j * ts + lax.broadcasted_iota(jnp.int32, (1, ts, 1), 1)
    mixed = []
    for g, w in enumerate(POOL_WINDOWS):
        cs = slice(g * POOL_GW, (g + 1) * POOL_GW)
        acc = ext_ref[:, POOL_HALO:POOL_HALO + ts, cs]
        for t in range(1, w):
            acc = acc + ext_ref[:, POOL_HALO - t:POOL_HALO - t + ts, cs]
        cnt = jnp.minimum(pos + 1, w).astype(F32)
        y = acc / cnt - u[:, :, cs]
        z = _dot(y.reshape(tm, POOL_GW).astype(BF16), wpool_ref[g]) * pscale_ref[:, cs]
        ga = gate_ref[:, :, cs].reshape(tm, POOL_GW)
        gb = gate_ref[:, :, D_MODEL + g * POOL_GW:D_MODEL + (g + 1) * POOL_GW].reshape(tm, POOL_GW)
        mixed.append((ga * z + gb * o_ref[:, :, cs].reshape(tm, POOL_GW)).astype(BF16))
    out = _dot(jnp.concatenate(mixed, axis=-1), wout_ref[...])
    x1 = x_ref[...].reshape(tm, D_MODEL) + _rows(g1_ref, bb, ts) * out
    x1_ref[...] = x1.reshape(bb, ts, D_MODEL)


def _mix(x, u, prev, gate, o, mod, lw, *, bb, ts, pos0, prompt):
    b, s, _ = x.shape
    tok = lambda n: pl.BlockSpec((bb, ts, n), lambda bi, j: (bi, j, 0))
    if prompt:
        r = ts // POOL_HALO
        prev_spec = pl.BlockSpec((bb, POOL_HALO, D_MODEL), lambda bi, j: (bi, jnp.maximum(j * r - 1, 0), 0))
    else:
        prev_spec = pl.BlockSpec((bb, POOL_CTX, D_MODEL), lambda bi, j: (bi, 0, 0))
    return pl.pallas_call(
        functools.partial(_mix_kernel, bb=bb, ts=ts, pos0=pos0, prompt=prompt),
        grid=(b // bb, s // ts),
        in_specs=[tok(D_MODEL), tok(D_MODEL), prev_spec, tok(2 * D_MODEL), tok(D_MODEL),
                  pl.BlockSpec((bb, 1, D_MODEL), lambda bi, j: (bi, 0, 2)),
                  _const_spec((len(POOL_WINDOWS), POOL_GW, POOL_GW)), _const_spec((1, D_MODEL)),
                  _const_spec((D_MODEL, D_MODEL))],
        out_specs=tok(D_MODEL),
        out_shape=jax.ShapeDtypeStruct((b, s, D_MODEL), F32),
        scratch_shapes=[pltpu.VMEM((bb, POOL_HALO + ts, D_MODEL), F32)],
        compiler_params=_params("parallel", "arbitrary"),
        name="mix_prompt" if prompt else "mix_sample",
    )(x, u, prev, gate, o, mod, lw["w_pool"], lw["pool_scale"], lw["w_out"])


FF_CHUNK = D_FF // 2
CONV_HALO = SUBLANES


def _ffn_kernel(x_ref, prev_ref, sh_ref, sc_ref, g2_ref, gf_ref, wup_ref, wconv_ref, bconv_ref, wdown_ref,
                gfin_ref, y_ref, cnew_ref, extg_ref, extv_ref, *, bb, ts, prompt, final):
    j = pl.program_id(1)
    tm = bb * ts
    x = x_ref[...].reshape(tm, D_MODEL)
    sc = _rows(sc_ref, bb, ts)
    sh = _rows(sh_ref, bb, ts)
    if prompt:
        xe = jnp.concatenate([prev_ref[0], x], axis=0)
    else:
        xe = x
    hb = (_rms(xe, gf_ref[...]) * (1.0 + sc) + sh).astype(BF16)
    acc = jnp.zeros((tm, D_MODEL), F32)
    for c in range(D_FF // FF_CHUNK):
        conv = []
        for part, ext_ref in ((0, extg_ref), (1, extv_ref)):
            lo = part * D_FF + c * FF_CHUNK
            cs = slice(lo, lo + FF_CHUNK)
            z = _dot(hb, wup_ref[:, cs])
            if prompt:
                keep = jnp.logical_or(j > 0, lax.broadcasted_iota(jnp.int32, (tm + CONV_HALO, 1), 0) >= CONV_HALO)
                ext_ref[0] = jnp.where(keep, z, 0.0)
            else:
                ext_ref[:, CONV_HALO:, :] = z.reshape(bb, ts, FF_CHUNK)
                ext_ref[:, CONV_HALO - (CONV_W - 1):CONV_HALO, :] = prev_ref[:, :, cs]
            out = bconv_ref[:, cs][None]
            for t in range(CONV_W):
                lo_r = CONV_HALO - (CONV_W - 1) + t
                out = out + ext_ref[:, lo_r:lo_r + ts, :] * wconv_ref[t:t + 1, cs][None]
            cnew_ref[:, :, cs] = ext_ref[:, CONV_HALO + ts - (CONV_W - 1):CONV_HALO + ts, :]
            conv.append(out.reshape(tm, FF_CHUNK))
        act = (jax.nn.silu(conv[0]) * conv[1]).astype(BF16)
        acc = acc + _dot(act, wdown_ref[c * FF_CHUNK:(c + 1) * FF_CHUNK, :])
    x2 = x + _rows(g2_ref, bb, ts) * acc
    if final:
        x2 = _rms(x2, gfin_ref[...])
    y_ref[...] = x2.reshape(bb, ts, D_MODEL)


def _ffn(x, prev, mod, lw, g_final, *, bb, ts, prompt, final):
    b, s, _ = x.shape
    tok = pl.BlockSpec((bb, ts, D_MODEL), lambda bi, j: (bi, j, 0))
    if prompt:
        r = ts // CONV_HALO
        prev_spec = pl.BlockSpec((bb, CONV_HALO, D_MODEL), lambda bi, j: (bi, jnp.maximum(j * r - 1, 0), 0))
        halo = CONV_HALO
    else:
        prev_spec = pl.BlockSpec((bb, CONV_W - 1, 2 * D_FF), lambda bi, j: (bi, 0, 0))
        halo = 0
    modspec = lambda k: pl.BlockSpec((bb, 1, D_MODEL), lambda bi, j: (bi, 0, k))
    return pl.pallas_call(
        functools.partial(_ffn_kernel, bb=bb, ts=ts, prompt=prompt, final=final),
        grid=(b // bb, s // ts),
        in_specs=[tok, prev_spec, modspec(3), modspec(4), modspec(5), _const_spec((1, D_MODEL)),
                  _const_spec((D_MODEL, 2 * D_FF)), _const_spec((CONV_W, 2 * D_FF)), _const_spec((1, 2 * D_FF)),
                  _const_spec((D_FF, D_MODEL)), _const_spec((1, D_MODEL))],
        out_specs=[tok, pl.BlockSpec((bb, CONV_W - 1, 2 * D_FF), lambda bi, j: (bi, 0, 0))],
        out_shape=[jax.ShapeDtypeStruct((b, s, D_MODEL), F32),
                   jax.ShapeDtypeStruct((b, CONV_W - 1, 2 * D_FF), F32)],
        scratch_shapes=[pltpu.VMEM((bb, CONV_HALO + ts, FF_CHUNK), F32),
                        pltpu.VMEM((bb, CONV_HALO + ts, FF_CHUNK), F32)],
        compiler_params=_params("parallel", "arbitrary"),
        name=("ffn_prompt" if prompt else "ffn_sample") + ("_final" if final else ""),
    )(x, prev, mod, mod, mod, lw["g_ffn"], lw["w_up"], lw["w_conv"], lw["b_conv"], lw["w_down"], g_final)


def _rope_tables(pos):
    freq = ROPE_THETA ** (-jnp.arange(ROPE_HALF, dtype=F32) / ROPE_HALF)
    ang = pos[:, None] * freq[None, :]
    cos, sin = jnp.cos(ang), jnp.sin(ang)
    z16 = jnp.zeros_like(cos)
    one = jnp.ones((pos.shape[0], QK_NOPE), F32)
    zero = jnp.zeros((pos.shape[0], QK_NOPE), F32)
    pad = jnp.zeros((pos.shape[0], LANES - QK_NOPE - QK_ROPE), F32)
    span0, span1, span2 = [cos, cos], [z16, sin], [-sin, z16]
    cat = lambda parts: jnp.concatenate(parts, axis=-1)
    tab_k = jnp.stack([cat(span0 + [pad] + span0 + [pad]), cat(span1 + [pad] + span1 + [pad]),
                       cat(span2 + [pad] + span2 + [pad])])
    tab_q = jnp.stack([cat([one] + span0 + [pad]), cat([zero] + span1 + [pad]), cat([zero] + span2 + [pad])])
    return tab_k, tab_q * Q_PRESCALE


def _prep_layer(l, w_in, g_attn, g_q, w_uq, g_kv, w_uk, w_uv, w_pool, pool_scale, w_out, g_ffn, w_up,
                w_conv, b_conv, w_down):
    wi = w_in[l]
    o2 = IN_A
    o3 = o2 + QK_ROPE
    kr = wi[:, o2:o3]
    zpad = jnp.zeros((D_MODEL, LANES // 2 - QK_ROPE), F32)
    w_in_p = jnp.concatenate([wi[:, :o2], kr, zpad, kr, zpad, wi[:, o3:]], axis=1).astype(BF16)
    hpad = HEAD_BLOCK - QK_NOPE - QK_ROPE
    w_q = jnp.pad(w_uq[l].reshape(Q_LORA, N_HEADS, QK_NOPE + QK_ROPE), ((0, 0), (0, 0), (0, hpad)))
    w_k = jnp.pad(w_uk[l], ((0, 0), (0, 0), (0, HEAD_BLOCK - QK_NOPE)))
    w_ukt = jnp.pad(jnp.transpose(w_uk[l], (1, 2, 0)), ((0, 0), (0, HEAD_BLOCK - QK_NOPE), (0, 0)))
    wv_heads = jnp.transpose(w_uv[l], (1, 0, 2))
    w_vup = jnp.stack([jnp.pad(wv_heads[h], ((0, 0), ((h % HEADS_PER_STEP) * V_HEAD,
                                                       (HEADS_PER_STEP - 1 - h % HEADS_PER_STEP) * V_HEAD)))
                       for h in range(N_HEADS)])
    return dict(
        w_in=w_in_p, g_attn=g_attn[l][None], g_q=g_q[l][None], g_kv=g_kv[l][None],
        w_q=w_q.reshape(Q_LORA, N_HEADS * HEAD_BLOCK).astype(BF16),
        w_k=w_k.reshape(KV_LORA, N_HEADS * HEAD_BLOCK).astype(BF16),
        w_v=jnp.pad(w_uv[l], ((0, 0), (0, 0), (0, HEAD_BLOCK - V_HEAD))).reshape(
            KV_LORA, N_HEADS * HEAD_BLOCK).astype(BF16),
        w_ukt=w_ukt.astype(BF16), w_vup=w_vup.astype(BF16),
        w_pool=w_pool[l].astype(BF16), pool_scale=pool_scale[l][None], w_out=w_out[l].astype(BF16),
        g_ffn=g_ffn[l][None], w_up=w_up[l].astype(BF16), w_conv=w_conv[l], b_conv=b_conv[l][None],
        w_down=w_down[l].astype(BF16))


def _pick(n, pref):
    t = min(n, pref)
    while n % t:
        t -= 1
    return t


def kernel(x_prompt, x_sample, cache_latent, state_pool, state_conv, page_table, c_prompt, c_sample, w_ada, b_ada, g_attn, w_in, g_q, w_uq, g_kv, w_uk, w_uv, w_pool, pool_scale, w_out, g_ffn, w_up, w_conv, b_conv, w_down, g_final):
    depth = w_ada.shape[0]
    bp, sp, _ = x_prompt.shape
    bs, ss, _ = x_sample.shape
    past_len = page_table.shape[1] * PAGE_SIZE

    mod_all = _ada(jnp.concatenate([c_prompt, c_sample], axis=0), w_ada.astype(BF16), b_ada[:, None, :])
    tabk_p, tabq_p = _rope_tables(jnp.arange(sp, dtype=F32))
    tabk_s, tabq_s = _rope_tables(past_len + jnp.arange(ss, dtype=F32))
    gfin = g_final[None]
    cache_t = jnp.swapaxes(cache_latent, 2, 3)

    ts_p = _pick(sp, 512)
    ts_ffn = _pick(sp, 512)
    bb_s = _pick(bs, 32)

    yp, ys = x_prompt, x_sample
    outs = [[] for _ in range(6)]
    for l in range(depth):
        lw = _prep_layer(l, w_in, g_attn, g_q, w_uq, g_kv, w_uk, w_uv, w_pool, pool_scale, w_out, g_ffn,
                         w_up, w_conv, b_conv, w_down)
        final = l == depth - 1
        mod_p = mod_all[l, :bp][:, None, :]
        mod_s = mod_all[l, bp:][:, None, :]

        cqn, lat, u, gate, kf, vf = _in_proj(yp.reshape(bp * sp, D_MODEL), mod_p, tabk_p, lw,
                                             n_batch=bp, seq=sp, bb=1, ts=ts_p, prompt=True)
        q = _q_proj(cqn, tabq_p, lw["w_q"], seq=sp, bb=1, ts=ts_p, prompt=True)
        o = _attn(q.reshape(bp, sp, -1), kf.reshape(bp, sp, -1), vf.reshape(bp, sp, -1), tq=ts_p)
        u3 = u.reshape(bp, sp, D_MODEL)
        x1 = _mix(yp, u3, u3, gate.reshape(bp, sp, -1), o, mod_p, lw, bb=1, ts=ts_p, pos0=0, prompt=True)
        yp, conv_p = _ffn(x1, x1, mod_p, lw, gfin, bb=1, ts=ts_ffn, prompt=True, final=final)
        outs[0].append(lat.reshape(bp, sp, LATENT))
        outs[1].append(u3[:, sp - POOL_CTX:, :])
        outs[2].append(conv_p)

        cqn, lat, u, gate = _in_proj(ys.reshape(bs * ss, D_MODEL), mod_s, tabk_s, lw,
                                     n_batch=bs, seq=ss, bb=bb_s, ts=ss, prompt=False)
        q = _q_proj(cqn, tabq_s, lw["w_q"], seq=ss, bb=bb_s, ts=ss, prompt=False)
        qlat, qrope = _absorb(q, lw["w_ukt"], n_seq=bs, ts=ss)
        lat3 = lat.reshape(bs, ss, LATENT)
        olat = _paged_attn(page_table, qlat, qrope, lat3, cache_t, l)
        o = _v_up(olat, lw["w_vup"])
        u3 = u.reshape(bs, ss, D_MODEL)
        x1 = _mix(ys, u3, state_pool[l], gate.reshape(bs, ss, -1), o.reshape(bs, ss, -1), mod_s, lw,
                  bb=bb_s, ts=ss, pos0=past_len, prompt=False)
        ys, conv_s = _ffn(x1, state_conv[l], mod_s, lw, gfin, bb=bb_s, ts=ss, prompt=False, final=final)
        outs[3].append(lat3)
        outs[4].append(jnp.concatenate([state_pool[l], u3], axis=1)[:, -POOL_CTX:, :])
        outs[5].append(conv_s)

    return (yp, ys) + tuple(jnp.stack(o) for o in outs)
```

```python
import functools

import jax
import jax.numpy as jnp
from jax import lax
from jax.experimental import pallas as pl
from jax.experimental.pallas import tpu as pltpu

F32 = jnp.float32
BF16 = jnp.bfloat16

D_MODEL = 1024
N_HEADS = 16
QK_NOPE = 64
QK_ROPE = 32
ROPE_HALF = QK_ROPE // 2
V_HEAD = 64
KV_LORA = 256
Q_LORA = 768
LATENT = KV_LORA + QK_ROPE
ROPE_THETA = 10000.0
ATTN_SCALE = (QK_NOPE + QK_ROPE) ** -0.5
Q_PRESCALE = ATTN_SCALE * 1.4426950408889634
POOL_WINDOWS = (2, 4, 8, 16)
POOL_GW = D_MODEL // len(POOL_WINDOWS)
POOL_CTX = 15
D_FF = 2816
CONV_W = 3
PAGE_SIZE = 128
EPS = 1e-6
NEG_INF = -1e30

LANES = 128
SUBLANES = 8
HEAD_BLOCK = LANES
POOL_HALO = 16
IN_A = Q_LORA + KV_LORA
IN_KR = IN_A + LANES
IN_U = IN_KR + D_MODEL
IN_END = IN_U + 2 * D_MODEL
VMEM_LIMIT = 56 * 1024 * 1024

NT_DIMS = (((1,), (1,)), ((), ()))


def _dot(a, b):
    return jnp.dot(a, b, preferred_element_type=F32)


def _dot_nt(a, b):
    return lax.dot_general(a, b, NT_DIMS, preferred_element_type=F32)


def _rms(x, g):
    return x * lax.rsqrt(jnp.mean(x * x, axis=-1, keepdims=True) + EPS) * g


def _rows(ref, bb, ts):
    v = ref[...]
    n = v.shape[-1]
    if bb == 1:
        return v[0]
    return jnp.broadcast_to(v, (bb, ts, n)).reshape(bb * ts, n)


def _table_rows(tab_ref, idx, bb, ts):
    t = tab_ref[idx]
    if bb == 1:
        return t
    return jnp.broadcast_to(t[None], (bb, ts, LANES)).reshape(bb * ts, LANES)


def _rope_block(blk, t0, t1, t2):
    return blk * t0 + pltpu.roll(blk, ROPE_HALF, 1) * t1 + pltpu.roll(blk, LANES - ROPE_HALF, 1) * t2


def _const_spec(shape):
    nd = len(shape)
    return pl.BlockSpec(shape, lambda *_: (0,) * nd, pipeline_mode=pl.Buffered(1))


def _params(*sem):
    return pltpu.CompilerParams(dimension_semantics=sem, vmem_limit_bytes=VMEM_LIMIT)


def _ada_kernel(c_ref, w_ref, b_ref, o_ref):
    o_ref[...] = _dot(c_ref[...].astype(BF16), w_ref[...]) + b_ref[...]


def _ada(c_all, w_ada, b_ada):
    depth, d, n = w_ada.shape
    rows = c_all.shape[0]
    tn = D_MODEL
    return pl.pallas_call(
        _ada_kernel,
        grid=(depth, n // tn),
        in_specs=[pl.BlockSpec((rows, d), lambda l, j: (0, 0)),
                  pl.BlockSpec((None, d, tn), lambda l, j: (l, 0, j)),
                  pl.BlockSpec((None, 1, tn), lambda l, j: (l, 0, j))],
        out_specs=pl.BlockSpec((None, rows, tn), lambda l, j: (l, 0, j)),
        out_shape=jax.ShapeDtypeStruct((depth, rows, n), F32),
        compiler_params=_params("parallel", "parallel"),
        name="ada",
    )(c_all, w_ada, b_ada)


def _in_proj_kernel(*refs, bb, ts, prompt):
    if prompt:
        (x_ref, sh_ref, sc_ref, tab_ref, ga_ref, gq_ref, gkv_ref, win_ref, wk_ref, wv_ref,
         cqn_ref, lat_ref, u_ref, gate_ref, k_ref, v_ref) = refs
    else:
        (x_ref, sh_ref, sc_ref, tab_ref, ga_ref, gq_ref, gkv_ref, win_ref,
         cqn_ref, lat_ref, u_ref, gate_ref) = refs
    x = x_ref[...]
    h = _rms(x, ga_ref[...]) * (1.0 + _rows(sc_ref, bb, ts)) + _rows(sh_ref, bb, ts)
    hb = h.astype(BF16)
    a = _dot(hb, win_ref[:, 0:IN_A])
    cqn_ref[...] = _rms(a[:, 0:Q_LORA], gq_ref[...]).astype(BF16)
    lat = _rms(a[:, Q_LORA:IN_A], gkv_ref[...])
    lat_ref[:, 0:KV_LORA] = lat
    kr = _dot(hb, win_ref[:, IN_A:IN_KR])
    krr = _rope_block(kr, _table_rows(tab_ref, 0, bb, ts), _table_rows(tab_ref, 1, bb, ts),
                      _table_rows(tab_ref, 2, bb, ts))
    lat_ref[:, KV_LORA:LATENT] = krr[:, 0:QK_ROPE]
    u_ref[...] = _dot(hb, win_ref[:, IN_KR:IN_U])
    gate_ref[...] = jax.nn.sigmoid(_dot(hb, win_ref[:, IN_U:IN_END])).astype(gate_ref.dtype)
    if prompt:
        latb = lat.astype(BF16)
        lane = lax.broadcasted_iota(jnp.int32, krr.shape, 1)
        kr_head = jnp.where(lane >= QK_NOPE, krr, 0.0)
        kf = _dot(latb, wk_ref[...])
        vf = _dot(latb, wv_ref[...])
        for hd in range(N_HEADS):
            sl = slice(hd * HEAD_BLOCK, (hd + 1) * HEAD_BLOCK)
            k_ref[:, sl] = (kf[:, sl] + kr_head).astype(BF16)
            v_ref[:, sl] = jnp.where(lane == V_HEAD, 1.0, vf[:, sl]).astype(BF16)


def _in_proj(x, mod, tab, lw, *, n_batch, seq, bb, ts, prompt):
    m = x.shape[0]
    tm = bb * ts
    nst = seq // ts
    if prompt:
        row = lambda i: (i, 0)
        modmap = lambda k: (lambda i: (i // nst, 0, k))
        tabmap = lambda i: (0, i % nst, 0)
    else:
        row = lambda i: (i, 0)
        modmap = lambda k: (lambda i: (i, 0, k))
        tabmap = lambda i: (0, 0, 0)
    in_specs = [pl.BlockSpec((tm, D_MODEL), row),
                pl.BlockSpec((bb, 1, D_MODEL), modmap(0)),
                pl.BlockSpec((bb, 1, D_MODEL), modmap(1)),
                pl.BlockSpec((3, ts, LANES), tabmap),
                _const_spec((1, D_MODEL)), _const_spec((1, Q_LORA)), _const_spec((1, KV_LORA)),
                _const_spec((D_MODEL, IN_END))]
    args = [x, mod, mod, tab, lw["g_attn"], lw["g_q"], lw["g_kv"], lw["w_in"]]
    out_specs = [pl.BlockSpec((tm, Q_LORA), row), pl.BlockSpec((tm, LATENT), row),
                 pl.BlockSpec((tm, D_MODEL), row), pl.BlockSpec((tm, 2 * D_MODEL), row)]
    out_shape = [jax.ShapeDtypeStruct((m, Q_LORA), BF16), jax.ShapeDtypeStruct((m, LATENT), F32),
                 jax.ShapeDtypeStruct((m, D_MODEL), F32),
                 jax.ShapeDtypeStruct((m, 2 * D_MODEL), BF16 if prompt else F32)]
    if prompt:
        in_specs += [_const_spec((KV_LORA, N_HEADS * HEAD_BLOCK))] * 2
        args += [lw["w_k"], lw["w_v"]]
        out_specs += [pl.BlockSpec((tm, N_HEADS * HEAD_BLOCK), row)] * 2
        out_shape += [jax.ShapeDtypeStruct((m, N_HEADS * HEAD_BLOCK), BF16)] * 2
    return pl.pallas_call(
        functools.partial(_in_proj_kernel, bb=bb, ts=ts, prompt=prompt),
        grid=(m // tm,), in_specs=in_specs, out_specs=out_specs, out_shape=out_shape,
        compiler_params=_params("parallel"),
        name="in_proj_prompt" if prompt else "in_proj_sample",
    )(*args)


def _q_proj_kernel(cqn_ref, tab_ref, wq_ref, q_ref, *, bb, ts):
    q = _dot(cqn_ref[...], wq_ref[...])
    t0, t1, t2 = (_table_rows(tab_ref, i, bb, ts) for i in range(3))
    for hd in range(N_HEADS):
        sl = slice(hd * HEAD_BLOCK, (hd + 1) * HEAD_BLOCK)
        q_ref[:, sl] = _rope_block(q[:, sl], t0, t1, t2).astype(BF16)


def _q_proj(cqn, tab, w_q, *, seq, bb, ts, prompt):
    m = cqn.shape[0]
    tm = bb * ts
    nst = seq // ts
    tabmap = (lambda i: (0, i % nst, 0)) if prompt else (lambda i: (0, 0, 0))
    return pl.pallas_call(
        functools.partial(_q_proj_kernel, bb=bb, ts=ts),
        grid=(m // tm,),
        in_specs=[pl.BlockSpec((tm, Q_LORA), lambda i: (i, 0)),
                  pl.BlockSpec((3, ts, LANES), tabmap),
                  _const_spec((Q_LORA, N_HEADS * HEAD_BLOCK))],
        out_specs=pl.BlockSpec((tm, N_HEADS * HEAD_BLOCK), lambda i: (i, 0)),
        out_shape=jax.ShapeDtypeStruct((m, N_HEADS * HEAD_BLOCK), BF16),
        compiler_params=_params("parallel"),
        name="q_proj_prompt" if prompt else "q_proj_sample",
    )(cqn, tab, w_q)


HEADS_PER_STEP = 2


ATTN_STRIP = 16


def _attn_groups(nq):
    if nq % 2:
        return [(g,) for g in range(nq)]
    return [(g, nq - 1 - g) for g in range(nq // 2)]


def _attn_kernel(q_ref, k_ref, v_ref, o_ref, s_ref, p_ref, m_ref, alpha_ref, acc_ref, *, tq, groups):
    reps = tq // LANES
    lane = lax.broadcasted_iota(jnp.int32, (tq, HEAD_BLOCK), 1)

    def run_group(group):
        tiles = [(slot, qi, j) for slot, qi in enumerate(group) for j in range(qi + 1)]
        for t, (slot, qi, j) in enumerate(tiles):
            buf = t % 2
            first, last = j == 0, j == qi
            qrows = slice(qi * tq, (qi + 1) * tq)
            krows = slice(j * tq, (j + 1) * tq)
            for hh in range(HEADS_PER_STEP):
                hs = slice(hh * HEAD_BLOCK, (hh + 1) * HEAD_BLOCK)
                s_ref[buf, hh] = _dot_nt(q_ref[0, qrows, hs], k_ref[0, krows, hs])
            for hh in range(HEADS_PER_STEP):
                hs = slice(hh * HEAD_BLOCK, (hh + 1) * HEAD_BLOCK)
                for r in range(tq // ATTN_STRIP):
                    rows = slice(r * ATTN_STRIP, (r + 1) * ATTN_STRIP)
                    s = s_ref[buf, hh, rows, :]
                    if last:
                        qpos = r * ATTN_STRIP + lax.broadcasted_iota(jnp.int32, (ATTN_STRIP, tq), 0)
                        kpos = lax.broadcasted_iota(jnp.int32, (ATTN_STRIP, tq), 1)
                        s = jnp.where(kpos <= qpos, s, NEG_INF)
                    mx = jnp.max(s, axis=-1, keepdims=True)
                    if first:
                        m_new = jnp.broadcast_to(mx, (ATTN_STRIP, LANES))
                    else:
                        m_prev = m_ref[slot, hh, rows, :]
                        m_new = jnp.maximum(m_prev, mx)
                        alpha_ref[buf, hh, rows, :] = jnp.exp2(m_prev - m_new)
                    if not last:
                        m_ref[slot, hh, rows, :] = m_new
                    p_ref[buf, hh, rows, :] = jnp.exp2(s - jnp.tile(m_new, (1, reps))).astype(BF16)
                pv = _dot(p_ref[buf, hh], v_ref[0, krows, hs])
                acc = pv if first else alpha_ref[buf, hh] * acc_ref[slot, hh] + pv
                if last:
                    out = acc / acc[:, V_HEAD:V_HEAD + 1]
                    if hh == 0:
                        out0 = out
                    else:
                        both = jnp.where(lane < V_HEAD, out0, pltpu.roll(out, V_HEAD, 1))
                        o_ref[0, qrows, :] = both.astype(BF16)
                else:
                    acc_ref[slot, hh] = acc

    if len(groups) == 1:
        run_group(groups[0])
    else:
        for g, group in enumerate(groups):
            pl.when(pl.program_id(2) == g)(functools.partial(run_group, group))


def _attn(q, k, v, *, tq):
    b, s, _ = q.shape
    hw = HEADS_PER_STEP * HEAD_BLOCK
    vw = HEADS_PER_STEP * V_HEAD
    groups = _attn_groups(s // tq)
    slots = len(groups[0])
    seq = pl.BlockSpec((1, s, hw), lambda bi, hp, g: (bi, 0, hp))
    return pl.pallas_call(
        functools.partial(_attn_kernel, tq=tq, groups=groups),
        grid=(b, N_HEADS // HEADS_PER_STEP, len(groups)),
        in_specs=[seq, seq, seq],
        out_specs=pl.BlockSpec((1, s, vw), lambda bi, hp, g: (bi, 0, hp)),
        out_shape=jax.ShapeDtypeStruct((b, s, N_HEADS * V_HEAD), BF16),
        scratch_shapes=[pltpu.VMEM((2, HEADS_PER_STEP, tq, tq), F32), pltpu.VMEM((2, HEADS_PER_STEP, tq, tq), BF16),
                        pltpu.VMEM((slots, HEADS_PER_STEP, tq, LANES), F32),
                        pltpu.VMEM((2, HEADS_PER_STEP, tq, LANES), F32),
                        pltpu.VMEM((slots, HEADS_PER_STEP, tq, HEAD_BLOCK), F32)],
        compiler_params=_params("parallel", "parallel", "arbitrary"),
        name="attn_prompt",
    )(q, k, v)


def _absorb_kernel(q_ref, w_ref, qlat_ref, qrope_ref, *, n_seq, ts):
    blk = q_ref[...]
    qlat = _dot(blk, w_ref[0])
    qlat_ref[...] = qlat.reshape(n_seq, ts, KV_LORA)
    rolled = pltpu.roll(blk.astype(F32), LANES - QK_NOPE, 1)
    qrope_ref[...] = rolled[:, 0:QK_ROPE].reshape(n_seq, ts, QK_ROPE)


def _absorb(q, w_ukt, *, n_seq, ts):
    m = q.shape[0]
    return pl.pallas_call(
        functools.partial(_absorb_kernel, n_seq=n_seq, ts=ts),
        grid=(N_HEADS,),
        in_specs=[pl.BlockSpec((m, HEAD_BLOCK), lambda h: (0, h)),
                  pl.BlockSpec((1, HEAD_BLOCK, KV_LORA), lambda h: (h, 0, 0))],
        out_specs=[pl.BlockSpec((n_seq, None, ts, KV_LORA), lambda h: (0, h, 0, 0)),
                   pl.BlockSpec((n_seq, None, ts, QK_ROPE), lambda h: (0, h, 0, 0))],
        out_shape=[jax.ShapeDtypeStruct((n_seq, N_HEADS, ts, KV_LORA), F32),
                   jax.ShapeDtypeStruct((n_seq, N_HEADS, ts, QK_ROPE), F32)],
        compiler_params=_params("parallel"),
        name="absorb",
    )(q, w_ukt)


PAGES_PER_CHUNK = 16


def _paged_attn_kernel(pt_ref, qlat_ref, qrope_ref, new_ref, cache_ref, o_ref, buf_ref, kt_ref, s_ref, sem,
                       *, n_pages, ts, layer):
    b = pl.program_id(0)
    nq = N_HEADS * ts
    slot = b % 2

    def page_copy(seq, j, slot_):
        page = pt_ref[seq * n_pages + j]
        return pltpu.make_async_copy(cache_ref.at[layer, page], buf_ref.at[slot_, j], sem.at[slot_])

    @pl.when(b == 0)
    def _():
        for j in range(n_pages):
            page_copy(0, j, 0).start()

    @pl.when(b + 1 < pl.num_programs(0))
    def _():
        for j in range(n_pages):
            page_copy(b + 1, j, 1 - slot).start()

    for j in range(n_pages):
        page_copy(b, j, slot).wait()
    ql = qlat_ref[0].reshape(nq, KV_LORA).astype(BF16)
    qr = qrope_ref[0].reshape(nq, QK_ROPE).astype(BF16)
    new = new_ref[0]
    nl = new[:, 0:KV_LORA].astype(BF16)
    s_new = _dot_nt(ql, nl) + _dot_nt(qr, new[:, KV_LORA:LATENT].astype(BF16))
    tok = lax.broadcasted_iota(jnp.int32, (nq, ts), 0) % ts
    key = lax.broadcasted_iota(jnp.int32, (nq, ts), 1)
    s_new = jnp.where(key <= tok, s_new, NEG_INF)
    m = jnp.max(s_new, axis=-1, keepdims=True)
    chunk_pages = min(n_pages, PAGES_PER_CHUNK)
    chunks = [slice(c * chunk_pages * PAGE_SIZE, (c + 1) * chunk_pages * PAGE_SIZE)
              for c in range(n_pages // chunk_pages)]
    for c, cols in enumerate(chunks):
        for j in range(c * chunk_pages, (c + 1) * chunk_pages):
            kt_ref[:, j * PAGE_SIZE:(j + 1) * PAGE_SIZE] = buf_ref[slot, j].astype(BF16)
        s = _dot(ql, kt_ref[0:KV_LORA, cols]) + _dot(qr, kt_ref[KV_LORA:LATENT, cols])
        s_ref[:, cols] = s
        m = jnp.maximum(m, jnp.max(s, axis=-1, keepdims=True))
    p_new = jnp.exp2(s_new - m)
    l = jnp.sum(p_new, axis=-1, keepdims=True)
    acc = _dot(p_new.astype(BF16), nl)
    for cols in chunks:
        p = jnp.exp2(s_ref[:, cols] - m)
        l = l + jnp.sum(p, axis=-1, keepdims=True)
        acc = acc + _dot_nt(p.astype(BF16), kt_ref[0:KV_LORA, cols])
    o = acc / l
    o_ref[0] = o.reshape(N_HEADS, ts, KV_LORA)


def _paged_attn(page_table, qlat, qrope, lat_new, cache_t, layer):
    n_seq, n_pages = page_table.shape
    ts = qlat.shape[2]

    grid_spec = pltpu.PrefetchScalarGridSpec(
        num_scalar_prefetch=1,
        grid=(n_seq,),
        in_specs=[pl.BlockSpec((1, N_HEADS, ts, KV_LORA), lambda b, pt: (b, 0, 0, 0)),
                  pl.BlockSpec((1, N_HEADS, ts, QK_ROPE), lambda b, pt: (b, 0, 0, 0)),
                  pl.BlockSpec((1, ts, LATENT), lambda b, pt: (b, 0, 0)),
                  pl.BlockSpec(memory_space=pltpu.HBM)],
        out_specs=pl.BlockSpec((1, N_HEADS, ts, KV_LORA), lambda b, pt: (b, 0, 0, 0)),
        scratch_shapes=[pltpu.VMEM((2, n_pages, LATENT, PAGE_SIZE), F32),
                        pltpu.VMEM((LATENT, n_pages * PAGE_SIZE), BF16),
                        pltpu.VMEM((N_HEADS * ts, n_pages * PAGE_SIZE), F32),
                        pltpu.SemaphoreType.DMA((2,))],
    )
    return pl.pallas_call(
        functools.partial(_paged_attn_kernel, n_pages=n_pages, ts=ts, layer=layer),
        grid_spec=grid_spec,
        out_shape=jax.ShapeDtypeStruct((n_seq, N_HEADS, ts, KV_LORA), F32),
        compiler_params=_params("arbitrary"),
        name="paged_attn",
    )(page_table.reshape(-1), qlat, qrope, lat_new, cache_t)


def _v_up_kernel(o_ref, w_ref, out_ref, *, n_seq, ts):
    acc = None
    for e in range(HEADS_PER_STEP):
        x = o_ref[:, e].reshape(n_seq * ts, KV_LORA).astype(BF16)
        d = _dot(x, w_ref[e])
        acc = d if acc is None else acc + d
    out_ref[...] = acc


def _v_up(olat, w_vup):
    n_seq, _, ts, _ = olat.shape
    vw = HEADS_PER_STEP * V_HEAD
    return pl.pallas_call(
        functools.partial(_v_up_kernel, n_seq=n_seq, ts=ts),
        grid=(N_HEADS // HEADS_PER_STEP,),
        in_specs=[pl.BlockSpec((n_seq, HEADS_PER_STEP, ts, KV_LORA), lambda p: (0, p, 0, 0)),
                  pl.BlockSpec((HEADS_PER_STEP, KV_LORA, vw), lambda p: (p, 0, 0))],
        out_specs=pl.BlockSpec((n_seq * ts, vw), lambda p: (0, p)),
        out_shape=jax.ShapeDtypeStruct((n_seq * ts, N_HEADS * V_HEAD), F32),
        compiler_params=_params("parallel"),
        name="v_up",
    )(olat, w_vup)


def _mix_kernel(x_ref, u_ref, prev_ref, gate_ref, o_ref, g1_ref, wpool_ref, pscale_ref, wout_ref,
                x1_ref, ext_ref, *, bb, ts, pos0, prompt):
    j = pl.program_id(1)
    tm = bb * ts
    u = u_ref[...]
    ext_ref[:, POOL_HALO:, :] = u
    if prompt:
        ext_ref[:, 0:POOL_HALO, :] = jnp.where(j > 0, prev_ref[...], 0.0)
    else:
        ext_ref[:, POOL_HALO - POOL_CTX:POOL_HALO, :] = prev_ref[...]
    pos = pos0 + j * ts + lax.broadcasted_iota(jnp.int32, (1, ts, 1), 1)
    mixed = []
    for g, w in enumerate(POOL_WINDOWS):
        cs = slice(g * POOL_GW, (g + 1) * POOL_GW)
        acc = ext_ref[:, POOL_HALO:POOL_HALO + ts, cs]
        for t in range(1, w):
            acc = acc + ext_ref[:, POOL_HALO - t:POOL_HALO - t + ts, cs]
        cnt = jnp.minimum(pos + 1, w).astype(F32)
        y = acc / cnt - u[:, :, cs]
        z = _dot(y.reshape(tm, POOL_GW).astype(BF16), wpool_ref[g]) * pscale_ref[:, cs]
        ga = gate_ref[:, :, cs].astype(F32).reshape(tm, POOL_GW)
        gb = gate_ref[:, :, D_MODEL + g * POOL_GW:D_MODEL + (g + 1) * POOL_GW].astype(F32).reshape(tm, POOL_GW)
        mixed.append((ga * z + gb * o_ref[:, :, cs].astype(F32).reshape(tm, POOL_GW)).astype(BF16))
    out = _dot(jnp.concatenate(mixed, axis=-1), wout_ref[...])
    x1 = x_ref[...].reshape(tm, D_MODEL) + _rows(g1_ref, bb, ts) * out
    x1_ref[...] = x1.reshape(bb, ts, D_MODEL)


def _mix(x, u, prev, gate, o, mod, lw, *, bb, ts, pos0, prompt):
    b, s, _ = x.shape
    tok = lambda n: pl.BlockSpec((bb, ts, n), lambda bi, j: (bi, j, 0))
    if prompt:
        r = ts // POOL_HALO
        prev_spec = pl.BlockSpec((bb, POOL_HALO, D_MODEL), lambda bi, j: (bi, jnp.maximum(j * r - 1, 0), 0))
    else:
        prev_spec = pl.BlockSpec((bb, POOL_CTX, D_MODEL), lambda bi, j: (bi, 0, 0))
    return pl.pallas_call(
        functools.partial(_mix_kernel, bb=bb, ts=ts, pos0=pos0, prompt=prompt),
        grid=(b // bb, s // ts),
        in_specs=[tok(D_MODEL), tok(D_MODEL), prev_spec, tok(2 * D_MODEL), tok(D_MODEL),
                  pl.BlockSpec((bb, 1, D_MODEL), lambda bi, j: (bi, 0, 2)),
                  _const_spec((len(POOL_WINDOWS), POOL_GW, POOL_GW)), _const_spec((1, D_MODEL)),
                  _const_spec((D_MODEL, D_MODEL))],
        out_specs=tok(D_MODEL),
        out_shape=jax.ShapeDtypeStruct((b, s, D_MODEL), F32),
        scratch_shapes=[pltpu.VMEM((bb, POOL_HALO + ts, D_MODEL), F32)],
        compiler_params=_params("parallel", "arbitrary"),
        name="mix_prompt" if prompt else "mix_sample",
    )(x, u, prev, gate, o, mod, lw["w_pool"], lw["pool_scale"], lw["w_out"])


FF_CHUNK = D_FF // 2
CONV_HALO = SUBLANES


def _ffn_kernel(x_ref, prev_ref, sh_ref, sc_ref, g2_ref, gf_ref, wup_ref, wconv_ref, bconv_ref, wdown_ref,
                gfin_ref, y_ref, cnew_ref, extg_ref, extv_ref, *, bb, ts, prompt, final):
    j = pl.program_id(1)
    tm = bb * ts
    x = x_ref[...].reshape(tm, D_MODEL)
    sc = _rows(sc_ref, bb, ts)
    sh = _rows(sh_ref, bb, ts)
    if prompt:
        xe = jnp.concatenate([prev_ref[0], x], axis=0)
    else:
        xe = x
    hb = (_rms(xe, gf_ref[...]) * (1.0 + sc) + sh).astype(BF16)
    acc = jnp.zeros((tm, D_MODEL), F32)
    for c in range(D_FF // FF_CHUNK):
        conv = []
        for part, ext_ref in ((0, extg_ref), (1, extv_ref)):
            lo = part * D_FF + c * FF_CHUNK
            cs = slice(lo, lo + FF_CHUNK)
            z = _dot(hb, wup_ref[:, cs])
            if prompt:
                keep = jnp.logical_or(j > 0, lax.broadcasted_iota(jnp.int32, (tm + CONV_HALO, 1), 0) >= CONV_HALO)
                ext_ref[0] = jnp.where(keep, z, 0.0)
            else:
                ext_ref[:, CONV_HALO:, :] = z.reshape(bb, ts, FF_CHUNK)
                ext_ref[:, CONV_HALO - (CONV_W - 1):CONV_HALO, :] = prev_ref[:, :, cs]
            out = bconv_ref[:, cs][None]
            for t in range(CONV_W):
                lo_r = CONV_HALO - (CONV_W - 1) + t
                out = out + ext_ref[:, lo_r:lo_r + ts, :] * wconv_ref[t:t + 1, cs][None]
            cnew_ref[:, :, cs] = ext_ref[:, CONV_HALO + ts - (CONV_W - 1):CONV_HALO + ts, :]
            conv.append(out.reshape(tm, FF_CHUNK))
        act = (jax.nn.silu(conv[0]) * conv[1]).astype(BF16)
        acc = acc + _dot(act, wdown_ref[c * FF_CHUNK:(c + 1) * FF_CHUNK, :])
    x2 = x + _rows(g2_ref, bb, ts) * acc
    if final:
        x2 = _rms(x2, gfin_ref[...])
    y_ref[...] = x2.reshape(bb, ts, D_MODEL)


def _ffn(x, prev, mod, lw, g_final, *, bb, ts, prompt, final):
    b, s, _ = x.shape
    tok = pl.BlockSpec((bb, ts, D_MODEL), lambda bi, j: (bi, j, 0))
    if prompt:
        r = ts // CONV_HALO
        prev_spec = pl.BlockSpec((bb, CONV_HALO, D_MODEL), lambda bi, j: (bi, jnp.maximum(j * r - 1, 0), 0))
        halo = CONV_HALO
    else:
        prev_spec = pl.BlockSpec((bb, CONV_W - 1, 2 * D_FF), lambda bi, j: (bi, 0, 0))
        halo = 0
    modspec = lambda k: pl.BlockSpec((bb, 1, D_MODEL), lambda bi, j: (bi, 0, k))
    return pl.pallas_call(
        functools.partial(_ffn_kernel, bb=bb, ts=ts, prompt=prompt, final=final),
        grid=(b // bb, s // ts),
        in_specs=[tok, prev_spec, modspec(3), modspec(4), modspec(5), _const_spec((1, D_MODEL)),
                  _const_spec((D_MODEL, 2 * D_FF)), _const_spec((CONV_W, 2 * D_FF)), _const_spec((1, 2 * D_FF)),
                  _const_spec((D_FF, D_MODEL)), _const_spec((1, D_MODEL))],
        out_specs=[tok, pl.BlockSpec((bb, CONV_W - 1, 2 * D_FF), lambda bi, j: (bi, 0, 0))],
        out_shape=[jax.ShapeDtypeStruct((b, s, D_MODEL), F32),
                   jax.ShapeDtypeStruct((b, CONV_W - 1, 2 * D_FF), F32)],
        scratch_shapes=[pltpu.VMEM((bb, CONV_HALO + ts, FF_CHUNK), F32),
                        pltpu.VMEM((bb, CONV_HALO + ts, FF_CHUNK), F32)],
        compiler_params=_params("parallel", "arbitrary"),
        name=("ffn_prompt" if prompt else "ffn_sample") + ("_final" if final else ""),
    )(x, prev, mod, mod, mod, lw["g_ffn"], lw["w_up"], lw["w_conv"], lw["b_conv"], lw["w_down"], g_final)


def _rope_tables(pos):
    freq = ROPE_THETA ** (-jnp.arange(ROPE_HALF, dtype=F32) / ROPE_HALF)
    ang = pos[:, None] * freq[None, :]
    cos, sin = jnp.cos(ang), jnp.sin(ang)
    z16 = jnp.zeros_like(cos)
    one = jnp.ones((pos.shape[0], QK_NOPE), F32)
    zero = jnp.zeros((pos.shape[0], QK_NOPE), F32)
    pad = jnp.zeros((pos.shape[0], LANES - QK_NOPE - QK_ROPE), F32)
    span0, span1, span2 = [cos, cos], [z16, sin], [-sin, z16]
    cat = lambda parts: jnp.concatenate(parts, axis=-1)
    tab_k = jnp.stack([cat(span0 + [pad] + span0 + [pad]), cat(span1 + [pad] + span1 + [pad]),
                       cat(span2 + [pad] + span2 + [pad])])
    tab_q = jnp.stack([cat([one] + span0 + [pad]), cat([zero] + span1 + [pad]), cat([zero] + span2 + [pad])])
    return tab_k, tab_q * Q_PRESCALE


def _prep_layer(l, w_in, g_attn, g_q, w_uq, g_kv, w_uk, w_uv, w_pool, pool_scale, w_out, g_ffn, w_up,
                w_conv, b_conv, w_down):
    wi = w_in[l]
    o2 = IN_A
    o3 = o2 + QK_ROPE
    kr = wi[:, o2:o3]
    zpad = jnp.zeros((D_MODEL, LANES // 2 - QK_ROPE), F32)
    w_in_p = jnp.concatenate([wi[:, :o2], kr, zpad, kr, zpad, wi[:, o3:]], axis=1).astype(BF16)
    hpad = HEAD_BLOCK - QK_NOPE - QK_ROPE
    w_q = jnp.pad(w_uq[l].reshape(Q_LORA, N_HEADS, QK_NOPE + QK_ROPE), ((0, 0), (0, 0), (0, hpad)))
    w_k = jnp.pad(w_uk[l], ((0, 0), (0, 0), (0, HEAD_BLOCK - QK_NOPE)))
    w_ukt = jnp.pad(jnp.transpose(w_uk[l], (1, 2, 0)), ((0, 0), (0, HEAD_BLOCK - QK_NOPE), (0, 0)))
    wv_heads = jnp.transpose(w_uv[l], (1, 0, 2))
    w_vup = jnp.stack([jnp.pad(wv_heads[h], ((0, 0), ((h % HEADS_PER_STEP) * V_HEAD,
                                                       (HEADS_PER_STEP - 1 - h % HEADS_PER_STEP) * V_HEAD)))
                       for h in range(N_HEADS)])
    return dict(
        w_in=w_in_p, g_attn=g_attn[l][None], g_q=g_q[l][None], g_kv=g_kv[l][None],
        w_q=w_q.reshape(Q_LORA, N_HEADS * HEAD_BLOCK).astype(BF16),
        w_k=w_k.reshape(KV_LORA, N_HEADS * HEAD_BLOCK).astype(BF16),
        w_v=jnp.pad(w_uv[l], ((0, 0), (0, 0), (0, HEAD_BLOCK - V_HEAD))).reshape(
            KV_LORA, N_HEADS * HEAD_BLOCK).astype(BF16),
        w_ukt=w_ukt.astype(BF16), w_vup=w_vup.astype(BF16),
        w_pool=w_pool[l].astype(BF16), pool_scale=pool_scale[l][None], w_out=w_out[l].astype(BF16),
        g_ffn=g_ffn[l][None], w_up=w_up[l].astype(BF16), w_conv=w_conv[l], b_conv=b_conv[l][None],
        w_down=w_down[l].astype(BF16))


def _pick(n, pref):
    t = min(n, pref)
    while n % t:
        t -= 1
    return t


def kernel(x_prompt, x_sample, cache_latent, state_pool, state_conv, page_table, c_prompt, c_sample, w_ada, b_ada, g_attn, w_in, g_q, w_uq, g_kv, w_uk, w_uv, w_pool, pool_scale, w_out, g_ffn, w_up, w_conv, b_conv, w_down, g_final):
    depth = w_ada.shape[0]
    bp, sp, _ = x_prompt.shape
    bs, ss, _ = x_sample.shape
    past_len = page_table.shape[1] * PAGE_SIZE

    mod_all = _ada(jnp.concatenate([c_prompt, c_sample], axis=0), w_ada.astype(BF16), b_ada[:, None, :])
    tabk_p, tabq_p = _rope_tables(jnp.arange(sp, dtype=F32))
    tabk_s, tabq_s = _rope_tables(past_len + jnp.arange(ss, dtype=F32))
    gfin = g_final[None]
    cache_t = jnp.swapaxes(cache_latent, 2, 3)

    ts_p = _pick(sp, 512)
    ts_ffn = _pick(sp, 512)
    bb_s = _pick(bs, 32)

    yp, ys = x_prompt, x_sample
    outs = [[] for _ in range(6)]
    for l in range(depth):
        lw = _prep_layer(l, w_in, g_attn, g_q, w_uq, g_kv, w_uk, w_uv, w_pool, pool_scale, w_out, g_ffn,
                         w_up, w_conv, b_conv, w_down)
        final = l == depth - 1
        mod_p = mod_all[l, :bp][:, None, :]
        mod_s = mod_all[l, bp:][:, None, :]

        cqn, lat, u, gate, kf, vf = _in_proj(yp.reshape(bp * sp, D_MODEL), mod_p, tabk_p, lw,
                                             n_batch=bp, seq=sp, bb=1, ts=ts_p, prompt=True)
        q = _q_proj(cqn, tabq_p, lw["w_q"], seq=sp, bb=1, ts=ts_p, prompt=True)
        o = _attn(q.reshape(bp, sp, -1), kf.reshape(bp, sp, -1), vf.reshape(bp, sp, -1), tq=ts_p)
        u3 = u.reshape(bp, sp, D_MODEL)
        x1 = _mix(yp, u3, u3, gate.reshape(bp, sp, -1), o, mod_p, lw, bb=1, ts=ts_p, pos0=0, prompt=True)
        yp, conv_p = _ffn(x1, x1, mod_p, lw, gfin, bb=1, ts=ts_ffn, prompt=True, final=final)
        outs[0].append(lat.reshape(bp, sp, LATENT))
        outs[1].append(u3[:, sp - POOL_CTX:, :])
        outs[2].append(conv_p)

        cqn, lat, u, gate = _in_proj(ys.reshape(bs * ss, D_MODEL), mod_s, tabk_s, lw,
                                     n_batch=bs, seq=ss, bb=bb_s, ts=ss, prompt=False)
        q = _q_proj(cqn, tabq_s, lw["w_q"], seq=ss, bb=bb_s, ts=ss, prompt=False)
        qlat, qrope = _absorb(q, lw["w_ukt"], n_seq=bs, ts=ss)
        lat3 = lat.reshape(bs, ss, LATENT)
        olat = _paged_attn(page_table, qlat, qrope, lat3, cache_t, l)
        o = _v_up(olat, lw["w_vup"])
        u3 = u.reshape(bs, ss, D_MODEL)
        x1 = _mix(ys, u3, state_pool[l], gate.reshape(bs, ss, -1), o.reshape(bs, ss, -1), mod_s, lw,
                  bb=bb_s, ts=ss, pos0=past_len, prompt=False)
        ys, conv_s = _ffn(x1, state_conv[l], mod_s, lw, gfin, bb=bb_s, ts=ss, prompt=False, final=final)
        outs[3].append(lat3)
        outs[4].append(jnp.concatenate([state_pool[l], u3], axis=1)[:, -POOL_CTX:, :])
        outs[5].append(conv_s)

    return (yp, ys) + tuple(jnp.stack(o) for o in outs)
```

```python
import functools

import jax
import jax.numpy as jnp
from jax import lax
from jax.experimental import pallas as pl
from jax.experimental.pallas import tpu as pltpu

F32 = jnp.float32
BF16 = jnp.bfloat16

D_MODEL = 1024
N_HEADS = 16
QK_NOPE = 64
QK_ROPE = 32
ROPE_HALF = QK_ROPE // 2
V_HEAD = 64
KV_LORA = 256
Q_LORA = 768
LATENT = KV_LORA + QK_ROPE
ROPE_THETA = 10000.0
ATTN_SCALE = (QK_NOPE + QK_ROPE) ** -0.5
Q_PRESCALE = ATTN_SCALE * 1.4426950408889634
POOL_WINDOWS = (2, 4, 8, 16)
POOL_GW = D_MODEL // len(POOL_WINDOWS)
POOL_CTX = 15
D_FF = 2816
CONV_W = 3
PAGE_SIZE = 128
EPS = 1e-6
NEG_INF = -1e30

LANES = 128
SUBLANES = 8
HEAD_BLOCK = LANES
POOL_HALO = 16
IN_A = Q_LORA + KV_LORA
IN_KR = IN_A + LANES
IN_U = IN_KR + D_MODEL
IN_END = IN_U + 2 * D_MODEL
VMEM_LIMIT = 56 * 1024 * 1024

NT_DIMS = (((1,), (1,)), ((), ()))


def _dot(a, b):
    return jnp.dot(a, b, preferred_element_type=F32)


def _dot_nt(a, b):
    return lax.dot_general(a, b, NT_DIMS, preferred_element_type=F32)


def _rms(x, g):
    return x * lax.rsqrt(jnp.mean(x * x, axis=-1, keepdims=True) + EPS) * g


def _rows(ref, bb, ts):
    v = ref[...]
    n = v.shape[-1]
    if bb == 1:
        return v[0]
    return jnp.broadcast_to(v, (bb, ts, n)).reshape(bb * ts, n)


def _table_rows(tab_ref, idx, bb, ts):
    t = tab_ref[idx]
    if bb == 1:
        return t
    return jnp.broadcast_to(t[None], (bb, ts, LANES)).reshape(bb * ts, LANES)


def _rope_block(blk, t0, t1):
    return blk * t0 + pltpu.roll(blk, LANES - ROPE_HALF, 1) * t1


def _const_spec(shape):
    nd = len(shape)
    return pl.BlockSpec(shape, lambda *_: (0,) * nd, pipeline_mode=pl.Buffered(1))


def _layer_spec(shape, l):
    nd = len(shape)
    return pl.BlockSpec((None,) + tuple(shape), lambda *_: (l,) + (0,) * nd, pipeline_mode=pl.Buffered(1))


def _params(*sem):
    return pltpu.CompilerParams(dimension_semantics=sem, vmem_limit_bytes=VMEM_LIMIT)


def _ada_kernel(c_ref, w_ref, b_ref, o_ref):
    o_ref[...] = _dot(c_ref[...].astype(BF16), w_ref[...]) + b_ref[...]


def _ada(c_all, w_ada, b_ada):
    depth, d, n = w_ada.shape
    rows = c_all.shape[0]
    tn = D_MODEL
    return pl.pallas_call(
        _ada_kernel,
        grid=(depth, n // tn),
        in_specs=[pl.BlockSpec((rows, d), lambda l, j: (0, 0)),
                  pl.BlockSpec((None, d, tn), lambda l, j: (l, 0, j)),
                  pl.BlockSpec((None, 1, tn), lambda l, j: (l, 0, j))],
        out_specs=pl.BlockSpec((None, rows, tn), lambda l, j: (l, 0, j)),
        out_shape=jax.ShapeDtypeStruct((depth, rows, n), F32),
        compiler_params=_params("parallel", "parallel"),
        name="ada",
    )(c_all, w_ada, b_ada)


def _in_proj_kernel(*refs, bb, ts, prompt):
    if prompt:
        (x_ref, sh_ref, sc_ref, tab_ref, ga_ref, gq_ref, gkv_ref, win_ref, wk_ref, wv_ref,
         cqn_ref, lat_ref, u_ref, gate_ref, k_ref, v_ref) = refs
    else:
        (x_ref, sh_ref, sc_ref, tab_ref, ga_ref, gq_ref, gkv_ref, win_ref,
         cqn_ref, lat_ref, u_ref, gate_ref) = refs
    x = x_ref[...]
    h = _rms(x, ga_ref[...]) * (1.0 + _rows(sc_ref, bb, ts)) + _rows(sh_ref, bb, ts)
    hb = h.astype(BF16)
    a = _dot(hb, win_ref[:, 0:IN_A])
    cqn_ref[...] = _rms(a[:, 0:Q_LORA], gq_ref[...]).astype(BF16)
    lat = _rms(a[:, Q_LORA:IN_A], gkv_ref[...])
    lat_ref[:, 0:KV_LORA] = lat
    kr = _dot(hb, win_ref[:, IN_A:IN_KR])
    krr = _rope_block(kr, _table_rows(tab_ref, 0, bb, ts), _table_rows(tab_ref, 1, bb, ts))
    lat_ref[:, KV_LORA:LATENT] = krr[:, 0:QK_ROPE]
    u_ref[...] = _dot(hb, win_ref[:, IN_KR:IN_U])
    gate_ref[...] = jax.nn.sigmoid(_dot(hb, win_ref[:, IN_U:IN_END])).astype(gate_ref.dtype)
    if prompt:
        latb = lat.astype(BF16)
        lane = lax.broadcasted_iota(jnp.int32, krr.shape, 1)
        kr_head = jnp.where(lane >= QK_NOPE, krr, 0.0)
        kf = _dot(latb, wk_ref[...])
        vf = _dot(latb, wv_ref[...])
        for hd in range(N_HEADS):
            sl = slice(hd * HEAD_BLOCK, (hd + 1) * HEAD_BLOCK)
            k_ref[:, sl] = (kf[:, sl] + kr_head).astype(BF16)
            v_ref[:, sl] = jnp.where(lane == V_HEAD, 1.0, vf[:, sl]).astype(BF16)


def _in_proj(x, mod, mod_row0, tab, wts, l, *, seq, bb, ts, prompt):
    m = x.shape[0]
    tm = bb * ts
    nst = seq // ts
    row = lambda i: (i, 0)
    if prompt:
        modmap = lambda k: (lambda i: (l, mod_row0 + i // nst, 0, k))
        tabmap = lambda i: (0, i % nst, 0)
    else:
        modmap = lambda k: (lambda i: (l, mod_row0 + i, 0, k))
        tabmap = lambda i: (0, 0, 0)
    in_specs = [pl.BlockSpec((tm, D_MODEL), row),
                pl.BlockSpec((None, bb, 1, D_MODEL), modmap(0)),
                pl.BlockSpec((None, bb, 1, D_MODEL), modmap(1)),
                pl.BlockSpec((2, ts, LANES), tabmap),
                _layer_spec((1, D_MODEL), l), _layer_spec((1, Q_LORA), l), _layer_spec((1, KV_LORA), l),
                _layer_spec((D_MODEL, IN_END), l)]
    args = [x, mod, mod, tab, wts["g_attn"], wts["g_q"], wts["g_kv"], wts["w_in"]]
    out_specs = [pl.BlockSpec((tm, Q_LORA), row), pl.BlockSpec((tm, LATENT), row),
                 pl.BlockSpec((tm, D_MODEL), row), pl.BlockSpec((tm, 2 * D_MODEL), row)]
    out_shape = [jax.ShapeDtypeStruct((m, Q_LORA), BF16), jax.ShapeDtypeStruct((m, LATENT), F32),
                 jax.ShapeDtypeStruct((m, D_MODEL), F32),
                 jax.ShapeDtypeStruct((m, 2 * D_MODEL), BF16 if prompt else F32)]
    if prompt:
        in_specs += [_layer_spec((KV_LORA, N_HEADS * HEAD_BLOCK), l)] * 2
        args += [wts["w_k"], wts["w_v"]]
        out_specs += [pl.BlockSpec((tm, N_HEADS * HEAD_BLOCK), row)] * 2
        out_shape += [jax.ShapeDtypeStruct((m, N_HEADS * HEAD_BLOCK), BF16)] * 2
    return pl.pallas_call(
        functools.partial(_in_proj_kernel, bb=bb, ts=ts, prompt=prompt),
        grid=(m // tm,), in_specs=in_specs, out_specs=out_specs, out_shape=out_shape,
        compiler_params=_params("parallel"),
        name="in_proj_prompt" if prompt else "in_proj_sample",
    )(*args)


def _q_proj_kernel(cqn_ref, tab_ref, wq_ref, q_ref, *, bb, ts):
    q = _dot(cqn_ref[...], wq_ref[...])
    t0, t1 = (_table_rows(tab_ref, i, bb, ts) for i in range(2))
    for hd in range(N_HEADS):
        sl = slice(hd * HEAD_BLOCK, (hd + 1) * HEAD_BLOCK)
        q_ref[:, sl] = _rope_block(q[:, sl], t0, t1).astype(BF16)


def _q_proj(cqn, tab, w_q, l, *, seq, bb, ts, prompt):
    m = cqn.shape[0]
    tm = bb * ts
    nst = seq // ts
    tabmap = (lambda i: (0, i % nst, 0)) if prompt else (lambda i: (0, 0, 0))
    return pl.pallas_call(
        functools.partial(_q_proj_kernel, bb=bb, ts=ts),
        grid=(m // tm,),
        in_specs=[pl.BlockSpec((tm, Q_LORA), lambda i: (i, 0)),
                  pl.BlockSpec((2, ts, LANES), tabmap),
                  _layer_spec((Q_LORA, N_HEADS * HEAD_BLOCK), l)],
        out_specs=pl.BlockSpec((tm, N_HEADS * HEAD_BLOCK), lambda i: (i, 0)),
        out_shape=jax.ShapeDtypeStruct((m, N_HEADS * HEAD_BLOCK), BF16),
        compiler_params=_params("parallel"),
        name="q_proj_prompt" if prompt else "q_proj_sample",
    )(cqn, tab, w_q)


HEADS_PER_STEP = 2


ATTN_STRIP = 16


def _attn_groups(nq):
    if nq % 2:
        return [(g,) for g in range(nq)]
    return [(g, nq - 1 - g) for g in range(nq // 2)]


def _attn_kernel(q_ref, k_ref, v_ref, o_ref, s_ref, p_ref, m_ref, alpha_ref, acc_ref, *, tq, groups):
    reps = tq // LANES
    lane = lax.broadcasted_iota(jnp.int32, (tq, HEAD_BLOCK), 1)

    def run_group(group):
        tiles = [(slot, qi, j) for slot, qi in enumerate(group) for j in range(qi + 1)]
        for t, (slot, qi, j) in enumerate(tiles):
            buf = t % 2
            first, last = j == 0, j == qi
            qrows = slice(qi * tq, (qi + 1) * tq)
            krows = slice(j * tq, (j + 1) * tq)
            for hh in range(HEADS_PER_STEP):
                hs = slice(hh * HEAD_BLOCK, (hh + 1) * HEAD_BLOCK)
                s_ref[buf, hh] = _dot_nt(q_ref[0, qrows, hs], k_ref[0, krows, hs])
            for hh in range(HEADS_PER_STEP):
                hs = slice(hh * HEAD_BLOCK, (hh + 1) * HEAD_BLOCK)
                for r in range(tq // ATTN_STRIP):
                    rows = slice(r * ATTN_STRIP, (r + 1) * ATTN_STRIP)
                    s = s_ref[buf, hh, rows, :]
                    if last:
                        qpos = r * ATTN_STRIP + lax.broadcasted_iota(jnp.int32, (ATTN_STRIP, tq), 0)
                        kpos = lax.broadcasted_iota(jnp.int32, (ATTN_STRIP, tq), 1)
                        s = jnp.where(kpos <= qpos, s, NEG_INF)
                    mx = jnp.max(s, axis=-1, keepdims=True)
                    if first:
                        m_new = jnp.broadcast_to(mx, (ATTN_STRIP, LANES))
                    else:
                        m_prev = m_ref[slot, hh, rows, :]
                        m_new = jnp.maximum(m_prev, mx)
                        alpha_ref[buf, hh, rows, :] = jnp.exp2(m_prev - m_new)
                    if not last:
                        m_ref[slot, hh, rows, :] = m_new
                    p_ref[buf, hh, rows, :] = jnp.exp2(s - jnp.tile(m_new, (1, reps))).astype(BF16)
                pv = _dot(p_ref[buf, hh], v_ref[0, krows, hs])
                acc = pv if first else alpha_ref[buf, hh] * acc_ref[slot, hh] + pv
                if last:
                    out = acc / acc[:, V_HEAD:V_HEAD + 1]
                    if hh == 0:
                        out0 = out
                    else:
                        both = jnp.where(lane < V_HEAD, out0, pltpu.roll(out, V_HEAD, 1))
                        o_ref[0, qrows, :] = both.astype(BF16)
                else:
                    acc_ref[slot, hh] = acc

    if len(groups) == 1:
        run_group(groups[0])
    else:
        for g, group in enumerate(groups):
            pl.when(pl.program_id(2) == g)(functools.partial(run_group, group))


def _attn(q, k, v, *, tq):
    b, s, _ = q.shape
    hw = HEADS_PER_STEP * HEAD_BLOCK
    vw = HEADS_PER_STEP * V_HEAD
    groups = _attn_groups(s // tq)
    slots = len(groups[0])
    seq = pl.BlockSpec((1, s, hw), lambda bi, hp, g: (bi, 0, hp))
    return pl.pallas_call(
        functools.partial(_attn_kernel, tq=tq, groups=groups),
        grid=(b, N_HEADS // HEADS_PER_STEP, len(groups)),
        in_specs=[seq, seq, seq],
        out_specs=pl.BlockSpec((1, s, vw), lambda bi, hp, g: (bi, 0, hp)),
        out_shape=jax.ShapeDtypeStruct((b, s, N_HEADS * V_HEAD), BF16),
        scratch_shapes=[pltpu.VMEM((2, HEADS_PER_STEP, tq, tq), F32), pltpu.VMEM((2, HEADS_PER_STEP, tq, tq), BF16),
                        pltpu.VMEM((slots, HEADS_PER_STEP, tq, LANES), F32),
                        pltpu.VMEM((2, HEADS_PER_STEP, tq, LANES), F32),
                        pltpu.VMEM((slots, HEADS_PER_STEP, tq, HEAD_BLOCK), F32)],
        compiler_params=_params("parallel", "parallel", "arbitrary"),
        name="attn_prompt",
    )(q, k, v)


def _absorb_kernel(q_ref, w_ref, qlat_ref, qrope_ref, *, n_seq, ts):
    blk = q_ref[...]
    qlat = _dot(blk, w_ref[0])
    qlat_ref[...] = qlat.reshape(n_seq, ts, KV_LORA)
    rolled = pltpu.roll(blk.astype(F32), LANES - QK_NOPE, 1)
    qrope_ref[...] = rolled[:, 0:QK_ROPE].reshape(n_seq, ts, QK_ROPE)


def _absorb(q, w_ukt, l, *, n_seq, ts):
    m = q.shape[0]
    return pl.pallas_call(
        functools.partial(_absorb_kernel, n_seq=n_seq, ts=ts),
        grid=(N_HEADS,),
        in_specs=[pl.BlockSpec((m, HEAD_BLOCK), lambda h: (0, h)),
                  pl.BlockSpec((None, 1, HEAD_BLOCK, KV_LORA), lambda h: (l, h, 0, 0))],
        out_specs=[pl.BlockSpec((n_seq, None, ts, KV_LORA), lambda h: (0, h, 0, 0)),
                   pl.BlockSpec((n_seq, None, ts, QK_ROPE), lambda h: (0, h, 0, 0))],
        out_shape=[jax.ShapeDtypeStruct((n_seq, N_HEADS, ts, KV_LORA), F32),
                   jax.ShapeDtypeStruct((n_seq, N_HEADS, ts, QK_ROPE), F32)],
        compiler_params=_params("parallel"),
        name="absorb",
    )(q, w_ukt)


PAGES_PER_CHUNK = 16


def _paged_attn_kernel(pt_ref, qlat_ref, qrope_ref, new_ref, cache_ref, o_ref, buf_ref, kt_ref, s_ref, sem,
                       *, n_pages, ts, layer):
    b = pl.program_id(0)
    nq = N_HEADS * ts
    slot = b % 2

    def page_copy(seq, j, slot_):
        page = pt_ref[seq * n_pages + j]
        return pltpu.make_async_copy(cache_ref.at[layer, page], buf_ref.at[slot_, j], sem.at[slot_])

    @pl.when(b == 0)
    def _():
        for j in range(n_pages):
            page_copy(0, j, 0).start()

    @pl.when(b + 1 < pl.num_programs(0))
    def _():
        for j in range(n_pages):
            page_copy(b + 1, j, 1 - slot).start()

    for j in range(n_pages):
        page_copy(b, j, slot).wait()
    ql = qlat_ref[0].reshape(nq, KV_LORA).astype(BF16)
    qr = qrope_ref[0].reshape(nq, QK_ROPE).astype(BF16)
    new = new_ref[0]
    nl = new[:, 0:KV_LORA].astype(BF16)
    s_new = _dot_nt(ql, nl) + _dot_nt(qr, new[:, KV_LORA:LATENT].astype(BF16))
    tok = lax.broadcasted_iota(jnp.int32, (nq, ts), 0) % ts
    key = lax.broadcasted_iota(jnp.int32, (nq, ts), 1)
    s_new = jnp.where(key <= tok, s_new, NEG_INF)
    m = jnp.max(s_new, axis=-1, keepdims=True)
    chunk_pages = min(n_pages, PAGES_PER_CHUNK)
    chunks = [slice(c * chunk_pages * PAGE_SIZE, (c + 1) * chunk_pages * PAGE_SIZE)
              for c in range(n_pages // chunk_pages)]
    for c, cols in enumerate(chunks):
        for j in range(c * chunk_pages, (c + 1) * chunk_pages):
            kt_ref[:, j * PAGE_SIZE:(j + 1) * PAGE_SIZE] = buf_ref[slot, j].astype(BF16)
        s = _dot(ql, kt_ref[0:KV_LORA, cols]) + _dot(qr, kt_ref[KV_LORA:LATENT, cols])
        s_ref[:, cols] = s
        m = jnp.maximum(m, jnp.max(s, axis=-1, keepdims=True))
    p_new = jnp.exp2(s_new - m)
    l = jnp.sum(p_new, axis=-1, keepdims=True)
    acc = _dot(p_new.astype(BF16), nl)
    for cols in chunks:
        p = jnp.exp2(s_ref[:, cols] - m)
        l = l + jnp.sum(p, axis=-1, keepdims=True)
        acc = acc + _dot_nt(p.astype(BF16), kt_ref[0:KV_LORA, cols])
    o = acc / l
    o_ref[0] = o.reshape(N_HEADS, ts, KV_LORA)


def _paged_attn(page_table, qlat, qrope, lat_new, cache_t, layer):
    n_seq, n_pages = page_table.shape
    ts = qlat.shape[2]

    grid_spec = pltpu.PrefetchScalarGridSpec(
        num_scalar_prefetch=1,
        grid=(n_seq,),
        in_specs=[pl.BlockSpec((1, N_HEADS, ts, KV_LORA), lambda b, pt: (b, 0, 0, 0)),
                  pl.BlockSpec((1, N_HEADS, ts, QK_ROPE), lambda b, pt: (b, 0, 0, 0)),
                  pl.BlockSpec((1, ts, LATENT), lambda b, pt: (b, 0, 0)),
                  pl.BlockSpec(memory_space=pltpu.HBM)],
        out_specs=pl.BlockSpec((1, N_HEADS, ts, KV_LORA), lambda b, pt: (b, 0, 0, 0)),
        scratch_shapes=[pltpu.VMEM((2, n_pages, LATENT, PAGE_SIZE), F32),
                        pltpu.VMEM((LATENT, n_pages * PAGE_SIZE), BF16),
                        pltpu.VMEM((N_HEADS * ts, n_pages * PAGE_SIZE), F32),
                        pltpu.SemaphoreType.DMA((2,))],
    )
    return pl.pallas_call(
        functools.partial(_paged_attn_kernel, n_pages=n_pages, ts=ts, layer=layer),
        grid_spec=grid_spec,
        out_shape=jax.ShapeDtypeStruct((n_seq, N_HEADS, ts, KV_LORA), F32),
        compiler_params=_params("arbitrary"),
        name="paged_attn",
    )(page_table.reshape(-1), qlat, qrope, lat_new, cache_t)


def _v_up_kernel(o_ref, w_ref, out_ref, *, n_seq, ts):
    acc = None
    for e in range(HEADS_PER_STEP):
        x = o_ref[:, e].reshape(n_seq * ts, KV_LORA).astype(BF16)
        d = _dot(x, w_ref[e])
        acc = d if acc is None else acc + d
    out_ref[...] = acc


def _v_up(olat, w_vup, l):
    n_seq, _, ts, _ = olat.shape
    vw = HEADS_PER_STEP * V_HEAD
    return pl.pallas_call(
        functools.partial(_v_up_kernel, n_seq=n_seq, ts=ts),
        grid=(N_HEADS // HEADS_PER_STEP,),
        in_specs=[pl.BlockSpec((n_seq, HEADS_PER_STEP, ts, KV_LORA), lambda p: (0, p, 0, 0)),
                  pl.BlockSpec((None, HEADS_PER_STEP, KV_LORA, vw), lambda p: (l, p, 0, 0))],
        out_specs=pl.BlockSpec((n_seq * ts, vw), lambda p: (0, p)),
        out_shape=jax.ShapeDtypeStruct((n_seq * ts, N_HEADS * V_HEAD), F32),
        compiler_params=_params("parallel"),
        name="v_up",
    )(olat, w_vup)


def _mix_kernel(x_ref, u_ref, prev_ref, gate_ref, o_ref, g1_ref, wpool_ref, pscale_ref, wout_ref,
                x1_ref, ext_ref, *, bb, ts, pos0, prompt):
    j = pl.program_id(1)
    tm = bb * ts
    u = u_ref[...]
    ext_ref[:, POOL_HALO:, :] = u
    if prompt:
        ext_ref[:, 0:POOL_HALO, :] = jnp.where(j > 0, prev_ref[...], 0.0)
    else:
        ext_ref[:, POOL_HALO - POOL_CTX:POOL_HALO, :] = prev_ref[...]
    pos = pos0 + j * ts + lax.broadcasted_iota(jnp.int32, (1, ts, 1), 1)
    mixed = []
    for g, w in enumerate(POOL_WINDOWS):
        cs = slice(g * POOL_GW, (g + 1) * POOL_GW)
        acc = ext_ref[:, POOL_HALO:POOL_HALO + ts, cs]
        for t in range(1, w):
            acc = acc + ext_ref[:, POOL_HALO - t:POOL_HALO - t + ts, cs]
        cnt = jnp.minimum(pos + 1, w).astype(F32)
        y = acc / cnt - u[:, :, cs]
        z = _dot(y.reshape(tm, POOL_GW).astype(BF16), wpool_ref[g]) * pscale_ref[:, cs]
        ga = gate_ref[:, :, cs].astype(F32).reshape(tm, POOL_GW)
        gb = gate_ref[:, :, D_MODEL + g * POOL_GW:D_MODEL + (g + 1) * POOL_GW].astype(F32).reshape(tm, POOL_GW)
        mixed.append((ga * z + gb * o_ref[:, :, cs].astype(F32).reshape(tm, POOL_GW)).astype(BF16))
    out = _dot(jnp.concatenate(mixed, axis=-1), wout_ref[...])
    x1 = x_ref[...].reshape(tm, D_MODEL) + _rows(g1_ref, bb, ts) * out
    x1_ref[...] = x1.reshape(bb, ts, D_MODEL)


def _mix(x, u, prev, gate, o, mod, mod_row0, wts, l, *, bb, ts, pos0, prompt):
    b, s, _ = x.shape
    tok = lambda n: pl.BlockSpec((bb, ts, n), lambda bi, j: (bi, j, 0))
    if prompt:
        r = ts // POOL_HALO
        prev_spec = pl.BlockSpec((bb, POOL_HALO, D_MODEL), lambda bi, j: (bi, jnp.maximum(j * r - 1, 0), 0))
    else:
        prev_spec = pl.BlockSpec((None, bb, POOL_CTX, D_MODEL), lambda bi, j: (l, bi, 0, 0))
    return pl.pallas_call(
        functools.partial(_mix_kernel, bb=bb, ts=ts, pos0=pos0, prompt=prompt),
        grid=(b // bb, s // ts),
        in_specs=[tok(D_MODEL), tok(D_MODEL), prev_spec, tok(2 * D_MODEL), tok(D_MODEL),
                  pl.BlockSpec((None, bb, 1, D_MODEL), lambda bi, j: (l, mod_row0 + bi, 0, 2)),
                  _layer_spec((len(POOL_WINDOWS), POOL_GW, POOL_GW), l), _layer_spec((1, D_MODEL), l),
                  _layer_spec((D_MODEL, D_MODEL), l)],
        out_specs=tok(D_MODEL),
        out_shape=jax.ShapeDtypeStruct((b, s, D_MODEL), F32),
        scratch_shapes=[pltpu.VMEM((bb, POOL_HALO + ts, D_MODEL), F32)],
        compiler_params=_params("parallel", "arbitrary"),
        name="mix_prompt" if prompt else "mix_sample",
    )(x, u, prev, gate, o, mod, wts["w_pool"], wts["pool_scale"], wts["w_out"])


FF_CHUNK = D_FF // 2
CONV_HALO = SUBLANES


def _ffn_kernel(x_ref, prev_ref, sh_ref, sc_ref, g2_ref, gf_ref, wup_ref, wconv_ref, bconv_ref, wdown_ref,
                gfin_ref, y_ref, cnew_ref, extg_ref, extv_ref, *, bb, ts, prompt, final):
    j = pl.program_id(1)
    tm = bb * ts
    x = x_ref[...].reshape(tm, D_MODEL)
    sc = _rows(sc_ref, bb, ts)
    sh = _rows(sh_ref, bb, ts)
    if prompt:
        xe = jnp.concatenate([prev_ref[0], x], axis=0)
    else:
        xe = x
    hb = (_rms(xe, gf_ref[...]) * (1.0 + sc) + sh).astype(BF16)
    acts = []
    for c in range(D_FF // FF_CHUNK):
        z2 = _dot(hb, wup_ref[:, 2 * c * FF_CHUNK:2 * (c + 1) * FF_CHUNK])
        conv = []
        for part, ext_ref in ((0, extg_ref), (1, extv_ref)):
            lo = part * D_FF + c * FF_CHUNK
            cs = slice(lo, lo + FF_CHUNK)
            z = z2[:, part * FF_CHUNK:(part + 1) * FF_CHUNK]
            if prompt:
                keep = jnp.logical_or(j > 0, lax.broadcasted_iota(jnp.int32, (tm + CONV_HALO, 1), 0) >= CONV_HALO)
                ext_ref[0] = jnp.where(keep, z, 0.0)
            else:
                ext_ref[:, CONV_HALO:, :] = z.reshape(bb, ts, FF_CHUNK)
                ext_ref[:, CONV_HALO - (CONV_W - 1):CONV_HALO, :] = prev_ref[:, :, cs]
            out = bconv_ref[:, cs][None]
            for t in range(CONV_W):
                lo_r = CONV_HALO - (CONV_W - 1) + t
                out = out + ext_ref[:, lo_r:lo_r + ts, :] * wconv_ref[t:t + 1, cs][None]
            cnew_ref[:, :, cs] = ext_ref[:, CONV_HALO + ts - (CONV_W - 1):CONV_HALO + ts, :]
            conv.append(out.reshape(tm, FF_CHUNK))
        acts.append((jax.nn.silu(conv[0]) * conv[1]).astype(BF16))
    acc = _dot(jnp.concatenate(acts, axis=-1), wdown_ref[...])
    x2 = x + _rows(g2_ref, bb, ts) * acc
    if final:
        x2 = _rms(x2, gfin_ref[...])
    y_ref[...] = x2.reshape(bb, ts, D_MODEL)


def _ffn(x, prev, mod, mod_row0, wts, l, g_final, *, bb, ts, prompt, final):
    b, s, _ = x.shape
    tok = pl.BlockSpec((bb, ts, D_MODEL), lambda bi, j: (bi, j, 0))
    if prompt:
        r = ts // CONV_HALO
        prev_spec = pl.BlockSpec((bb, CONV_HALO, D_MODEL), lambda bi, j: (bi, jnp.maximum(j * r - 1, 0), 0))
    else:
        prev_spec = pl.BlockSpec((None, bb, CONV_W - 1, 2 * D_FF), lambda bi, j: (l, bi, 0, 0))
    modspec = lambda k: pl.BlockSpec((None, bb, 1, D_MODEL), lambda bi, j: (l, mod_row0 + bi, 0, k))
    return pl.pallas_call(
        functools.partial(_ffn_kernel, bb=bb, ts=ts, prompt=prompt, final=final),
        grid=(b // bb, s // ts),
        in_specs=[tok, prev_spec, modspec(3), modspec(4), modspec(5), _layer_spec((1, D_MODEL), l),
                  _layer_spec((D_MODEL, 2 * D_FF), l), _layer_spec((CONV_W, 2 * D_FF), l),
                  _layer_spec((1, 2 * D_FF), l), _layer_spec((D_FF, D_MODEL), l), _const_spec((1, D_MODEL))],
        out_specs=[tok, pl.BlockSpec((bb, CONV_W - 1, 2 * D_FF), lambda bi, j: (bi, 0, 0))],
        out_shape=[jax.ShapeDtypeStruct((b, s, D_MODEL), F32),
                   jax.ShapeDtypeStruct((b, CONV_W - 1, 2 * D_FF), F32)],
        scratch_shapes=[pltpu.VMEM((bb, CONV_HALO + ts, FF_CHUNK), F32),
                        pltpu.VMEM((bb, CONV_HALO + ts, FF_CHUNK), F32)],
        compiler_params=_params("parallel", "arbitrary"),
        name=("ffn_prompt" if prompt else "ffn_sample") + ("_final" if final else ""),
    )(x, prev, mod, mod, mod, wts["g_ffn"], wts["w_up"], wts["w_conv"], wts["b_conv"], wts["w_down"], g_final)


def _rope_tables(pos):
    freq = ROPE_THETA ** (-jnp.arange(ROPE_HALF, dtype=F32) / ROPE_HALF)
    ang = pos[:, None] * freq[None, :]
    cos, sin = jnp.cos(ang), jnp.sin(ang)
    z16 = jnp.zeros_like(cos)
    one = jnp.ones((pos.shape[0], QK_NOPE), F32)
    zero = jnp.zeros((pos.shape[0], QK_NOPE), F32)
    span0, span1 = [cos, cos, z16, z16], [-sin, sin, z16, z16]
    cat = lambda parts: jnp.concatenate(parts, axis=-1)
    tab_k = jnp.stack([cat(span0 + span0), cat(span1 + span1)])
    tab_q = jnp.stack([cat([one] + span0), cat([zero] + span1)])
    return tab_k, tab_q * Q_PRESCALE


def _prep_weights(w_in, g_attn, g_q, w_uq, g_kv, w_uk, w_uv, w_pool, pool_scale, w_out, g_ffn, w_up, w_conv, b_conv,
                  w_down):
    depth = w_in.shape[0]
    o2 = IN_A
    o3 = o2 + QK_ROPE
    kr = w_in[:, :, o2:o3]
    span = [kr, kr[:, :, :ROPE_HALF], jnp.zeros((depth, D_MODEL, ROPE_HALF), F32)]
    w_in_p = jnp.concatenate([w_in[:, :, :o2]] + span + span + [w_in[:, :, o3:]], axis=2).astype(BF16)
    wq = w_uq.reshape(depth, Q_LORA, N_HEADS, QK_NOPE + QK_ROPE)
    w_q = jnp.concatenate([wq, wq[..., QK_NOPE:QK_NOPE + ROPE_HALF],
                           jnp.zeros((depth, Q_LORA, N_HEADS, ROPE_HALF), F32)], axis=-1)
    w_k = jnp.pad(w_uk, ((0, 0), (0, 0), (0, 0), (0, HEAD_BLOCK - QK_NOPE)))
    w_v = jnp.pad(w_uv, ((0, 0), (0, 0), (0, 0), (0, HEAD_BLOCK - V_HEAD)))
    w_ukt = jnp.pad(jnp.transpose(w_uk, (0, 2, 3, 1)), ((0, 0), (0, 0), (0, HEAD_BLOCK - QK_NOPE), (0, 0)))
    wv_heads = jnp.transpose(w_uv, (0, 2, 1, 3))
    zeros_v = jnp.zeros_like(wv_heads)
    even = (jnp.arange(N_HEADS) % HEADS_PER_STEP == 0)[None, :, None, None]
    w_vup = jnp.concatenate([jnp.where(even, wv_heads, zeros_v), jnp.where(even, zeros_v, wv_heads)], axis=-1)
    chunks = []
    for c in range(D_FF // FF_CHUNK):
        chunks += [w_up[:, :, c * FF_CHUNK:(c + 1) * FF_CHUNK],
                   w_up[:, :, D_FF + c * FF_CHUNK:D_FF + (c + 1) * FF_CHUNK]]
    return dict(
        w_in=w_in_p, g_attn=g_attn[:, None], g_q=g_q[:, None], g_kv=g_kv[:, None],
        w_q=w_q.reshape(depth, Q_LORA, N_HEADS * HEAD_BLOCK).astype(BF16),
        w_k=w_k.reshape(depth, KV_LORA, N_HEADS * HEAD_BLOCK).astype(BF16),
        w_v=w_v.reshape(depth, KV_LORA, N_HEADS * HEAD_BLOCK).astype(BF16),
        w_ukt=w_ukt.astype(BF16), w_vup=w_vup.astype(BF16),
        w_pool=w_pool.astype(BF16), pool_scale=pool_scale[:, None], w_out=w_out.astype(BF16),
        g_ffn=g_ffn[:, None], w_up=jnp.concatenate(chunks, axis=2).astype(BF16), w_conv=w_conv,
        b_conv=b_conv[:, None], w_down=w_down.astype(BF16))


def _pick(n, pref):
    t = min(n, pref)
    while n % t:
        t -= 1
    return t


def kernel(x_prompt, x_sample, cache_latent, state_pool, state_conv, page_table, c_prompt, c_sample, w_ada, b_ada, g_attn, w_in, g_q, w_uq, g_kv, w_uk, w_uv, w_pool, pool_scale, w_out, g_ffn, w_up, w_conv, b_conv, w_down, g_final):
    depth = w_ada.shape[0]
    bp, sp, _ = x_prompt.shape
    bs, ss, _ = x_sample.shape
    past_len = page_table.shape[1] * PAGE_SIZE

    ts_p = _pick(sp, 512)
    bb_s = _pick(bs, 32)
    mod = _ada(jnp.concatenate([c_sample, c_prompt], axis=0), w_ada.astype(BF16), b_ada[:, None, :])
    mod = mod.reshape(depth, bs + bp, 1, -1)
    row0_s, row0_p = 0, bs
    tabk_p, tabq_p = _rope_tables(jnp.arange(sp, dtype=F32))
    tabk_s, tabq_s = _rope_tables(past_len + jnp.arange(ss, dtype=F32))
    gfin = g_final[None]
    cache_t = jnp.swapaxes(cache_latent, 2, 3)
    wts = _prep_weights(w_in, g_attn, g_q, w_uq, g_kv, w_uk, w_uv, w_pool, pool_scale, w_out, g_ffn, w_up, w_conv,
                        b_conv, w_down)

    yp, ys = x_prompt, x_sample
    outs = [[] for _ in range(6)]
    for l in range(depth):
        final = l == depth - 1

        cqn, lat, u, gate, kf, vf = _in_proj(yp.reshape(bp * sp, D_MODEL), mod, row0_p, tabk_p, wts, l,
                                             seq=sp, bb=1, ts=ts_p, prompt=True)
        q = _q_proj(cqn, tabq_p, wts["w_q"], l, seq=sp, bb=1, ts=ts_p, prompt=True)
        o = _attn(q.reshape(bp, sp, -1), kf.reshape(bp, sp, -1), vf.reshape(bp, sp, -1), tq=ts_p)
        u3 = u.reshape(bp, sp, D_MODEL)
        x1 = _mix(yp, u3, u3, gate.reshape(bp, sp, -1), o, mod, row0_p, wts, l,
                  bb=1, ts=ts_p, pos0=0, prompt=True)
        yp, conv_p = _ffn(x1, x1, mod, row0_p, wts, l, gfin, bb=1, ts=ts_p, prompt=True, final=final)
        outs[0].append(lat.reshape(bp, sp, LATENT))
        outs[1].append(u3[:, sp - POOL_CTX:, :])
        outs[2].append(conv_p)

        cqn, lat, u, gate = _in_proj(ys.reshape(bs * ss, D_MODEL), mod, row0_s, tabk_s, wts, l,
                                     seq=ss, bb=bb_s, ts=ss, prompt=False)
        q = _q_proj(cqn, tabq_s, wts["w_q"], l, seq=ss, bb=bb_s, ts=ss, prompt=False)
        qlat, qrope = _absorb(q, wts["w_ukt"], l, n_seq=bs, ts=ss)
        lat3 = lat.reshape(bs, ss, LATENT)
        olat = _paged_attn(page_table, qlat, qrope, lat3, cache_t, l)
        o = _v_up(olat, wts["w_vup"], l)
        u3 = u.reshape(bs, ss, D_MODEL)
        x1 = _mix(ys, u3, state_pool, gate.reshape(bs, ss, -1), o.reshape(bs, ss, -1), mod, row0_s, wts, l,
                  bb=bb_s, ts=ss, pos0=past_len, prompt=False)
        ys, conv_s = _ffn(x1, state_conv, mod, row0_s, wts, l, gfin, bb=bb_s, ts=ss, prompt=False, final=final)
        outs[3].append(lat3)
        outs[4].append(jnp.concatenate([state_pool[l], u3], axis=1)[:, -POOL_CTX:, :])
        outs[5].append(conv_s)

    return (yp, ys) + tuple(jnp.stack(o) for o in outs)
```

```python
import functools

import jax
import jax.numpy as jnp
from jax import lax
from jax.experimental import pallas as pl
from jax.experimental.pallas import tpu as pltpu

F32 = jnp.float32
BF16 = jnp.bfloat16

D_MODEL = 1024
N_HEADS = 16
QK_NOPE = 64
QK_ROPE = 32
ROPE_HALF = QK_ROPE // 2
V_HEAD = 64
KV_LORA = 256
Q_LORA = 768
LATENT = KV_LORA + QK_ROPE
ROPE_THETA = 10000.0
ATTN_SCALE = (QK_NOPE + QK_ROPE) ** -0.5
Q_PRESCALE = ATTN_SCALE * 1.4426950408889634
POOL_WINDOWS = (2, 4, 8, 16)
POOL_GW = D_MODEL // len(POOL_WINDOWS)
POOL_CTX = 15
D_FF = 2816
CONV_W = 3
PAGE_SIZE = 128
EPS = 1e-6
NEG_INF = -1e30

LANES = 128
SUBLANES = 8
HEAD_BLOCK = LANES
POOL_HALO = 16
IN_A = Q_LORA + KV_LORA
IN_KR = IN_A + LANES
IN_U = IN_KR + D_MODEL
IN_END = IN_U + 2 * D_MODEL
VMEM_LIMIT = 56 * 1024 * 1024

NT_DIMS = (((1,), (1,)), ((), ()))


def _dot(a, b):
    return jnp.dot(a, b, preferred_element_type=F32)


def _dot_nt(a, b):
    return lax.dot_general(a, b, NT_DIMS, preferred_element_type=F32)


def _rms(x, g):
    return x * lax.rsqrt(jnp.mean(x * x, axis=-1, keepdims=True) + EPS) * g


def _rows(ref, bb, ts):
    v = ref[...]
    n = v.shape[-1]
    if bb == 1:
        return v[0]
    return jnp.broadcast_to(v, (bb, ts, n)).reshape(bb * ts, n)


def _table_rows(tab_ref, idx, bb, ts):
    t = tab_ref[idx]
    if bb == 1:
        return t
    return jnp.broadcast_to(t[None], (bb, ts, LANES)).reshape(bb * ts, LANES)


def _rope_block(blk, t0, t1):
    return blk * t0 + pltpu.roll(blk, LANES - ROPE_HALF, 1) * t1


def _const_spec(shape):
    nd = len(shape)
    return pl.BlockSpec(shape, lambda *_: (0,) * nd, pipeline_mode=pl.Buffered(1))


def _layer_spec(shape, l):
    nd = len(shape)
    return pl.BlockSpec((None,) + tuple(shape), lambda *_: (l,) + (0,) * nd, pipeline_mode=pl.Buffered(1))


def _params(*sem):
    return pltpu.CompilerParams(dimension_semantics=sem, vmem_limit_bytes=VMEM_LIMIT)


def _ada_kernel(c_ref, w_ref, b_ref, o_ref):
    o_ref[...] = _dot(c_ref[...].astype(BF16), w_ref[...]) + b_ref[...]


def _ada(c_all, w_ada, b_ada):
    depth, d, n = w_ada.shape
    rows = c_all.shape[0]
    tn = D_MODEL
    return pl.pallas_call(
        _ada_kernel,
        grid=(depth, n // tn),
        in_specs=[pl.BlockSpec((rows, d), lambda l, j: (0, 0)),
                  pl.BlockSpec((None, d, tn), lambda l, j: (l, 0, j)),
                  pl.BlockSpec((None, 1, tn), lambda l, j: (l, 0, j))],
        out_specs=pl.BlockSpec((None, rows, tn), lambda l, j: (l, 0, j)),
        out_shape=jax.ShapeDtypeStruct((depth, rows, n), F32),
        compiler_params=_params("parallel", "parallel"),
        name="ada",
    )(c_all, w_ada, b_ada)


def _in_proj_kernel(*refs, bb, ts, prompt):
    if prompt:
        (x_ref, sh_ref, sc_ref, tab_ref, ga_ref, gq_ref, gkv_ref, win_ref, wk_ref, wv_ref,
         cqn_ref, lat_ref, u_ref, gate_ref, k_ref, v_ref) = refs
    else:
        (x_ref, sh_ref, sc_ref, tab_ref, ga_ref, gq_ref, gkv_ref, win_ref,
         cqn_ref, lat_ref, u_ref, gate_ref) = refs
    x = x_ref[...]
    h = _rms(x, ga_ref[...]) * (1.0 + _rows(sc_ref, bb, ts)) + _rows(sh_ref, bb, ts)
    hb = h.astype(BF16)
    a = _dot(hb, win_ref[:, 0:IN_A])
    cqn_ref[...] = _rms(a[:, 0:Q_LORA], gq_ref[...]).astype(BF16)
    lat = _rms(a[:, Q_LORA:IN_A], gkv_ref[...])
    lat_ref[:, 0:KV_LORA] = lat
    kr = _dot(hb, win_ref[:, IN_A:IN_KR])
    krr = _rope_block(kr, _table_rows(tab_ref, 0, bb, ts), _table_rows(tab_ref, 1, bb, ts))
    lat_ref[:, KV_LORA:LATENT] = krr[:, 0:QK_ROPE]
    u_ref[...] = _dot(hb, win_ref[:, IN_KR:IN_U])
    gate_ref[...] = jax.nn.sigmoid(_dot(hb, win_ref[:, IN_U:IN_END])).astype(gate_ref.dtype)
    if prompt:
        latb = lat.astype(BF16)
        lane = lax.broadcasted_iota(jnp.int32, krr.shape, 1)
        kr_head = jnp.where(lane >= QK_NOPE, krr, 0.0)
        kf = _dot(latb, wk_ref[...])
        vf = _dot(latb, wv_ref[...])
        for hd in range(N_HEADS):
            sl = slice(hd * HEAD_BLOCK, (hd + 1) * HEAD_BLOCK)
            k_ref[:, sl] = (kf[:, sl] + kr_head).astype(BF16)
            v_ref[:, sl] = jnp.where(lane == V_HEAD, 1.0, vf[:, sl]).astype(BF16)


def _in_proj(x, mod, mod_row0, tab, wts, l, *, seq, bb, ts, prompt):
    m = x.shape[0]
    tm = bb * ts
    nst = seq // ts
    row = lambda i: (i, 0)
    if prompt:
        modmap = lambda k: (lambda i: (l, mod_row0 + i // nst, 0, k))
        tabmap = lambda i: (0, i % nst, 0)
    else:
        modmap = lambda k: (lambda i: (l, mod_row0 + i, 0, k))
        tabmap = lambda i: (0, 0, 0)
    in_specs = [pl.BlockSpec((tm, D_MODEL), row),
                pl.BlockSpec((None, bb, 1, D_MODEL), modmap(0)),
                pl.BlockSpec((None, bb, 1, D_MODEL), modmap(1)),
                pl.BlockSpec((2, ts, LANES), tabmap),
                _layer_spec((1, D_MODEL), l), _layer_spec((1, Q_LORA), l), _layer_spec((1, KV_LORA), l),
                _layer_spec((D_MODEL, IN_END), l)]
    args = [x, mod, mod, tab, wts["g_attn"], wts["g_q"], wts["g_kv"], wts["w_in"]]
    out_specs = [pl.BlockSpec((tm, Q_LORA), row), pl.BlockSpec((tm, LATENT), row),
                 pl.BlockSpec((tm, D_MODEL), row), pl.BlockSpec((tm, 2 * D_MODEL), row)]
    out_shape = [jax.ShapeDtypeStruct((m, Q_LORA), BF16), jax.ShapeDtypeStruct((m, LATENT), F32),
                 jax.ShapeDtypeStruct((m, D_MODEL), F32),
                 jax.ShapeDtypeStruct((m, 2 * D_MODEL), BF16 if prompt else F32)]
    if prompt:
        in_specs += [_layer_spec((KV_LORA, N_HEADS * HEAD_BLOCK), l)] * 2
        args += [wts["w_k"], wts["w_v"]]
        out_specs += [pl.BlockSpec((tm, N_HEADS * HEAD_BLOCK), row)] * 2
        out_shape += [jax.ShapeDtypeStruct((m, N_HEADS * HEAD_BLOCK), BF16)] * 2
    return pl.pallas_call(
        functools.partial(_in_proj_kernel, bb=bb, ts=ts, prompt=prompt),
        grid=(m // tm,), in_specs=in_specs, out_specs=out_specs, out_shape=out_shape,
        compiler_params=_params("parallel"),
        name="in_proj_prompt" if prompt else "in_proj_sample",
    )(*args)


def _q_proj_kernel(cqn_ref, tab_ref, wq_ref, q_ref, *, bb, ts):
    q = _dot(cqn_ref[...], wq_ref[...])
    t0, t1 = (_table_rows(tab_ref, i, bb, ts) for i in range(2))
    for hd in range(N_HEADS):
        sl = slice(hd * HEAD_BLOCK, (hd + 1) * HEAD_BLOCK)
        q_ref[:, sl] = _rope_block(q[:, sl], t0, t1).astype(BF16)


def _q_proj(cqn, tab, w_q, l, *, seq, bb, ts, prompt):
    m = cqn.shape[0]
    tm = bb * ts
    nst = seq // ts
    tabmap = (lambda i: (0, i % nst, 0)) if prompt else (lambda i: (0, 0, 0))
    return pl.pallas_call(
        functools.partial(_q_proj_kernel, bb=bb, ts=ts),
        grid=(m // tm,),
        in_specs=[pl.BlockSpec((tm, Q_LORA), lambda i: (i, 0)),
                  pl.BlockSpec((2, ts, LANES), tabmap),
                  _layer_spec((Q_LORA, N_HEADS * HEAD_BLOCK), l)],
        out_specs=pl.BlockSpec((tm, N_HEADS * HEAD_BLOCK), lambda i: (i, 0)),
        out_shape=jax.ShapeDtypeStruct((m, N_HEADS * HEAD_BLOCK), BF16),
        compiler_params=_params("parallel"),
        name="q_proj_prompt" if prompt else "q_proj_sample",
    )(cqn, tab, w_q)


HEADS_PER_STEP = 2


ATTN_STRIP = 16


def _attn_groups(nq):
    if nq % 2:
        return [(g,) for g in range(nq)]
    return [(g, nq - 1 - g) for g in range(nq // 2)]


def _attn_kernel(q_ref, k_ref, v_ref, o_ref, s_ref, p_ref, m_ref, alpha_ref, acc_ref, *, tq, groups):
    reps = tq // LANES
    lane = lax.broadcasted_iota(jnp.int32, (tq, HEAD_BLOCK), 1)

    def run_group(group):
        tiles = [(slot, qi, j) for slot, qi in enumerate(group) for j in range(qi + 1)]
        for t, (slot, qi, j) in enumerate(tiles):
            buf = t % 2
            first, last = j == 0, j == qi
            qrows = slice(qi * tq, (qi + 1) * tq)
            krows = slice(j * tq, (j + 1) * tq)
            for hh in range(HEADS_PER_STEP):
                hs = slice(hh * HEAD_BLOCK, (hh + 1) * HEAD_BLOCK)
                s_ref[buf, hh] = _dot_nt(q_ref[0, qrows, hs], k_ref[0, krows, hs])
            for hh in range(HEADS_PER_STEP):
                hs = slice(hh * HEAD_BLOCK, (hh + 1) * HEAD_BLOCK)
                for r in range(tq // ATTN_STRIP):
                    rows = slice(r * ATTN_STRIP, (r + 1) * ATTN_STRIP)
                    s = s_ref[buf, hh, rows, :]
                    if last:
                        qpos = r * ATTN_STRIP + lax.broadcasted_iota(jnp.int32, (ATTN_STRIP, tq), 0)
                        kpos = lax.broadcasted_iota(jnp.int32, (ATTN_STRIP, tq), 1)
                        s = jnp.where(kpos <= qpos, s, NEG_INF)
                    mx = jnp.max(s, axis=-1, keepdims=True)
                    if first:
                        m_new = jnp.broadcast_to(mx, (ATTN_STRIP, LANES))
                    else:
                        m_prev = m_ref[slot, hh, rows, :]
                        m_new = jnp.maximum(m_prev, mx)
                        alpha_ref[buf, hh, rows, :] = jnp.exp2(m_prev - m_new)
                    if not last:
                        m_ref[slot, hh, rows, :] = m_new
                    p_ref[buf, hh, rows, :] = jnp.exp2(s - jnp.tile(m_new, (1, reps))).astype(BF16)
                pv = _dot(p_ref[buf, hh], v_ref[0, krows, hs])
                acc = pv if first else alpha_ref[buf, hh] * acc_ref[slot, hh] + pv
                if last:
                    out = acc / acc[:, V_HEAD:V_HEAD + 1]
                    if hh == 0:
                        out0 = out
                    else:
                        both = jnp.where(lane < V_HEAD, out0, pltpu.roll(out, V_HEAD, 1))
                        o_ref[0, qrows, :] = both.astype(BF16)
                else:
                    acc_ref[slot, hh] = acc

    if len(groups) == 1:
        run_group(groups[0])
    else:
        for g, group in enumerate(groups):
            pl.when(pl.program_id(2) == g)(functools.partial(run_group, group))


def _attn(q, k, v, *, tq):
    b, s, _ = q.shape
    hw = HEADS_PER_STEP * HEAD_BLOCK
    vw = HEADS_PER_STEP * V_HEAD
    groups = _attn_groups(s // tq)
    slots = len(groups[0])
    seq = pl.BlockSpec((1, s, hw), lambda bi, hp, g: (bi, 0, hp))
    return pl.pallas_call(
        functools.partial(_attn_kernel, tq=tq, groups=groups),
        grid=(b, N_HEADS // HEADS_PER_STEP, len(groups)),
        in_specs=[seq, seq, seq],
        out_specs=pl.BlockSpec((1, s, vw), lambda bi, hp, g: (bi, 0, hp)),
        out_shape=jax.ShapeDtypeStruct((b, s, N_HEADS * V_HEAD), BF16),
        scratch_shapes=[pltpu.VMEM((2, HEADS_PER_STEP, tq, tq), F32), pltpu.VMEM((2, HEADS_PER_STEP, tq, tq), BF16),
                        pltpu.VMEM((slots, HEADS_PER_STEP, tq, LANES), F32),
                        pltpu.VMEM((2, HEADS_PER_STEP, tq, LANES), F32),
                        pltpu.VMEM((slots, HEADS_PER_STEP, tq, HEAD_BLOCK), F32)],
        compiler_params=_params("parallel", "parallel", "arbitrary"),
        name="attn_prompt",
    )(q, k, v)


def _absorb_kernel(q_ref, w_ref, qlat_ref, qrope_ref, *, n_seq, ts):
    blk = q_ref[...]
    qlat = _dot(blk, w_ref[0])
    qlat_ref[...] = qlat.reshape(n_seq, ts, KV_LORA)
    rolled = pltpu.roll(blk.astype(F32), LANES - QK_NOPE, 1)
    qrope_ref[...] = rolled[:, 0:QK_ROPE].reshape(n_seq, ts, QK_ROPE)


def _absorb(q, w_ukt, l, *, n_seq, ts):
    m = q.shape[0]
    return pl.pallas_call(
        functools.partial(_absorb_kernel, n_seq=n_seq, ts=ts),
        grid=(N_HEADS,),
        in_specs=[pl.BlockSpec((m, HEAD_BLOCK), lambda h: (0, h)),
                  pl.BlockSpec((None, 1, HEAD_BLOCK, KV_LORA), lambda h: (l, h, 0, 0))],
        out_specs=[pl.BlockSpec((n_seq, None, ts, KV_LORA), lambda h: (0, h, 0, 0)),
                   pl.BlockSpec((n_seq, None, ts, QK_ROPE), lambda h: (0, h, 0, 0))],
        out_shape=[jax.ShapeDtypeStruct((n_seq, N_HEADS, ts, KV_LORA), F32),
                   jax.ShapeDtypeStruct((n_seq, N_HEADS, ts, QK_ROPE), F32)],
        compiler_params=_params("parallel"),
        name="absorb",
    )(q, w_ukt)


PAGES_PER_CHUNK = 16


def _paged_attn_kernel(pt_ref, qlat_ref, qrope_ref, new_ref, cache_ref, o_ref, buf_ref, kt_ref, s_ref, sem,
                       *, n_pages, ts, layer):
    b = pl.program_id(0)
    nq = N_HEADS * ts
    slot = b % 2

    def page_copy(seq, j, slot_):
        page = pt_ref[seq * n_pages + j]
        return pltpu.make_async_copy(cache_ref.at[layer, page], buf_ref.at[slot_, j], sem.at[slot_])

    @pl.when(b == 0)
    def _():
        for j in range(n_pages):
            page_copy(0, j, 0).start()

    @pl.when(b + 1 < pl.num_programs(0))
    def _():
        for j in range(n_pages):
            page_copy(b + 1, j, 1 - slot).start()

    for j in range(n_pages):
        page_copy(b, j, slot).wait()
    ql = qlat_ref[0].reshape(nq, KV_LORA).astype(BF16)
    qr = qrope_ref[0].reshape(nq, QK_ROPE).astype(BF16)
    new = new_ref[0]
    nl = new[:, 0:KV_LORA].astype(BF16)
    s_new = _dot_nt(ql, nl) + _dot_nt(qr, new[:, KV_LORA:LATENT].astype(BF16))
    tok = lax.broadcasted_iota(jnp.int32, (nq, ts), 0) % ts
    key = lax.broadcasted_iota(jnp.int32, (nq, ts), 1)
    s_new = jnp.where(key <= tok, s_new, NEG_INF)
    m = jnp.max(s_new, axis=-1, keepdims=True)
    chunk_pages = min(n_pages, PAGES_PER_CHUNK)
    chunks = [slice(c * chunk_pages * PAGE_SIZE, (c + 1) * chunk_pages * PAGE_SIZE)
              for c in range(n_pages // chunk_pages)]
    for c, cols in enumerate(chunks):
        for j in range(c * chunk_pages, (c + 1) * chunk_pages):
            kt_ref[:, j * PAGE_SIZE:(j + 1) * PAGE_SIZE] = buf_ref[slot, j].astype(BF16)
        s = _dot(ql, kt_ref[0:KV_LORA, cols]) + _dot(qr, kt_ref[KV_LORA:LATENT, cols])
        s_ref[:, cols] = s
        m = jnp.maximum(m, jnp.max(s, axis=-1, keepdims=True))
    p_new = jnp.exp2(s_new - m)
    l = jnp.sum(p_new, axis=-1, keepdims=True)
    acc = _dot(p_new.astype(BF16), nl)
    for cols in chunks:
        p = jnp.exp2(s_ref[:, cols] - m)
        l = l + jnp.sum(p, axis=-1, keepdims=True)
        acc = acc + _dot_nt(p.astype(BF16), kt_ref[0:KV_LORA, cols])
    o = acc / l
    o_ref[0] = o.reshape(N_HEADS, ts, KV_LORA)


def _paged_attn(page_table, qlat, qrope, lat_new, cache_t, layer):
    n_seq, n_pages = page_table.shape
    ts = qlat.shape[2]

    grid_spec = pltpu.PrefetchScalarGridSpec(
        num_scalar_prefetch=1,
        grid=(n_seq,),
        in_specs=[pl.BlockSpec((1, N_HEADS, ts, KV_LORA), lambda b, pt: (b, 0, 0, 0)),
                  pl.BlockSpec((1, N_HEADS, ts, QK_ROPE), lambda b, pt: (b, 0, 0, 0)),
                  pl.BlockSpec((1, ts, LATENT), lambda b, pt: (b, 0, 0)),
                  pl.BlockSpec(memory_space=pltpu.HBM)],
        out_specs=pl.BlockSpec((1, N_HEADS, ts, KV_LORA), lambda b, pt: (b, 0, 0, 0)),
        scratch_shapes=[pltpu.VMEM((2, n_pages, LATENT, PAGE_SIZE), F32),
                        pltpu.VMEM((LATENT, n_pages * PAGE_SIZE), BF16),
                        pltpu.VMEM((N_HEADS * ts, n_pages * PAGE_SIZE), F32),
                        pltpu.SemaphoreType.DMA((2,))],
    )
    return pl.pallas_call(
        functools.partial(_paged_attn_kernel, n_pages=n_pages, ts=ts, layer=layer),
        grid_spec=grid_spec,
        out_shape=jax.ShapeDtypeStruct((n_seq, N_HEADS, ts, KV_LORA), F32),
        compiler_params=_params("arbitrary"),
        name="paged_attn",
    )(page_table.reshape(-1), qlat, qrope, lat_new, cache_t)


def _v_up_kernel(o_ref, w_ref, out_ref, *, n_seq, ts):
    acc = None
    for e in range(HEADS_PER_STEP):
        x = o_ref[:, e].reshape(n_seq * ts, KV_LORA).astype(BF16)
        d = _dot(x, w_ref[e])
        acc = d if acc is None else acc + d
    out_ref[...] = acc


def _v_up(olat, w_vup, l):
    n_seq, _, ts, _ = olat.shape
    vw = HEADS_PER_STEP * V_HEAD
    return pl.pallas_call(
        functools.partial(_v_up_kernel, n_seq=n_seq, ts=ts),
        grid=(N_HEADS // HEADS_PER_STEP,),
        in_specs=[pl.BlockSpec((n_seq, HEADS_PER_STEP, ts, KV_LORA), lambda p: (0, p, 0, 0)),
                  pl.BlockSpec((None, HEADS_PER_STEP, KV_LORA, vw), lambda p: (l, p, 0, 0))],
        out_specs=pl.BlockSpec((n_seq * ts, vw), lambda p: (0, p)),
        out_shape=jax.ShapeDtypeStruct((n_seq * ts, N_HEADS * V_HEAD), F32),
        compiler_params=_params("parallel"),
        name="v_up",
    )(olat, w_vup)


def _mix_kernel(x_ref, u_ref, prev_ref, gate_ref, o_ref, g1_ref, wpool_ref, pscale_ref, wout_ref,
                x1_ref, ext_ref, *, bb, ts, pos0, prompt):
    j = pl.program_id(1)
    tm = bb * ts
    u = u_ref[...]
    ext_ref[:, POOL_HALO:, :] = u
    if prompt:
        ext_ref[:, 0:POOL_HALO, :] = jnp.where(j > 0, prev_ref[...], 0.0)
    else:
        for t in range(POOL_CTX):
            ext_ref[:, POOL_HALO - POOL_CTX + t, :] = prev_ref[t]
    pos = pos0 + j * ts + lax.broadcasted_iota(jnp.int32, (1, ts, 1), 1)
    mixed = []
    for g, w in enumerate(POOL_WINDOWS):
        cs = slice(g * POOL_GW, (g + 1) * POOL_GW)
        acc = ext_ref[:, POOL_HALO:POOL_HALO + ts, cs]
        for t in range(1, w):
            acc = acc + ext_ref[:, POOL_HALO - t:POOL_HALO - t + ts, cs]
        cnt = jnp.minimum(pos + 1, w).astype(F32)
        y = acc / cnt - u[:, :, cs]
        z = _dot(y.reshape(tm, POOL_GW).astype(BF16), wpool_ref[g]) * pscale_ref[:, cs]
        ga = gate_ref[:, :, cs].astype(F32).reshape(tm, POOL_GW)
        gb = gate_ref[:, :, D_MODEL + g * POOL_GW:D_MODEL + (g + 1) * POOL_GW].astype(F32).reshape(tm, POOL_GW)
        mixed.append((ga * z + gb * o_ref[:, :, cs].astype(F32).reshape(tm, POOL_GW)).astype(BF16))
    out = _dot(jnp.concatenate(mixed, axis=-1), wout_ref[...])
    x1 = x_ref[...].reshape(tm, D_MODEL) + _rows(g1_ref, bb, ts) * out
    x1_ref[...] = x1.reshape(bb, ts, D_MODEL)


def _mix(x, u, prev, gate, o, mod, mod_row0, wts, l, *, bb, ts, pos0, prompt):
    b, s, _ = x.shape
    tok = lambda n: pl.BlockSpec((bb, ts, n), lambda bi, j: (bi, j, 0))
    if prompt:
        r = ts // POOL_HALO
        prev_spec = pl.BlockSpec((bb, POOL_HALO, D_MODEL), lambda bi, j: (bi, jnp.maximum(j * r - 1, 0), 0))
    else:
        prev_spec = pl.BlockSpec((None, POOL_CTX, bb, D_MODEL), lambda bi, j: (l, 0, bi, 0))
    return pl.pallas_call(
        functools.partial(_mix_kernel, bb=bb, ts=ts, pos0=pos0, prompt=prompt),
        grid=(b // bb, s // ts),
        in_specs=[tok(D_MODEL), tok(D_MODEL), prev_spec, tok(2 * D_MODEL), tok(D_MODEL),
                  pl.BlockSpec((None, bb, 1, D_MODEL), lambda bi, j: (l, mod_row0 + bi, 0, 2)),
                  _layer_spec((len(POOL_WINDOWS), POOL_GW, POOL_GW), l), _layer_spec((1, D_MODEL), l),
                  _layer_spec((D_MODEL, D_MODEL), l)],
        out_specs=tok(D_MODEL),
        out_shape=jax.ShapeDtypeStruct((b, s, D_MODEL), F32),
        scratch_shapes=[pltpu.VMEM((bb, POOL_HALO + ts, D_MODEL), F32)],
        compiler_params=_params("parallel", "arbitrary"),
        name="mix_prompt" if prompt else "mix_sample",
    )(x, u, prev, gate, o, mod, wts["w_pool"], wts["pool_scale"], wts["w_out"])


MXU_DIM = 256
FF_CHUNKS = (6 * MXU_DIM, 5 * MXU_DIM)
FF_STARTS = tuple(sum(FF_CHUNKS[:c]) for c in range(len(FF_CHUNKS)))
assert sum(FF_CHUNKS) == D_FF
CONV_HALO = SUBLANES


def _ffn_kernel(x_ref, prev_ref, sh_ref, sc_ref, g2_ref, gf_ref, wup_ref, wconv_ref, bconv_ref, wdown_ref,
                gfin_ref, y_ref, cnew_ref, extg_ref, extv_ref, *, bb, ts, prompt, final):
    j = pl.program_id(1)
    tm = bb * ts
    x = x_ref[...].reshape(tm, D_MODEL)
    sc = _rows(sc_ref, bb, ts)
    sh = _rows(sh_ref, bb, ts)
    if prompt:
        xe = jnp.concatenate([prev_ref[0], x], axis=0)
    else:
        xe = x
    hb = (_rms(xe, gf_ref[...]) * (1.0 + sc) + sh).astype(BF16)
    acc = jnp.zeros((tm, D_MODEL), F32)
    for start, width in zip(FF_STARTS, FF_CHUNKS):
        conv = []
        for part, ext_ref in ((0, extg_ref), (1, extv_ref)):
            cs = slice(part * D_FF + start, part * D_FF + start + width)
            z = _dot(hb, wup_ref[:, cs])
            if prompt:
                keep = jnp.logical_or(j > 0, lax.broadcasted_iota(jnp.int32, (tm + CONV_HALO, 1), 0) >= CONV_HALO)
                ext_ref[0, :, 0:width] = jnp.where(keep, z, 0.0)
            else:
                ext_ref[:, CONV_HALO:, 0:width] = z.reshape(bb, ts, width)
                ext_ref[:, CONV_HALO - (CONV_W - 1):CONV_HALO, 0:width] = prev_ref[:, :, cs]
            out = bconv_ref[:, cs][None]
            for t in range(CONV_W):
                lo_r = CONV_HALO - (CONV_W - 1) + t
                out = out + ext_ref[:, lo_r:lo_r + ts, 0:width] * wconv_ref[t:t + 1, cs][None]
            cnew_ref[:, :, cs] = ext_ref[:, CONV_HALO + ts - (CONV_W - 1):CONV_HALO + ts, 0:width]
            conv.append(out.reshape(tm, width))
        act = (jax.nn.silu(conv[0]) * conv[1]).astype(BF16)
        acc = acc + _dot(act, wdown_ref[start:start + width, :])
    x2 = x + _rows(g2_ref, bb, ts) * acc
    if final:
        x2 = _rms(x2, gfin_ref[...])
    y_ref[...] = x2.reshape(bb, ts, D_MODEL)


def _ffn(x, prev, mod, mod_row0, wts, l, g_final, *, bb, ts, prompt, final):
    b, s, _ = x.shape
    tok = pl.BlockSpec((bb, ts, D_MODEL), lambda bi, j: (bi, j, 0))
    if prompt:
        r = ts // CONV_HALO
        prev_spec = pl.BlockSpec((bb, CONV_HALO, D_MODEL), lambda bi, j: (bi, jnp.maximum(j * r - 1, 0), 0))
    else:
        prev_spec = pl.BlockSpec((None, bb, CONV_W - 1, 2 * D_FF), lambda bi, j: (l, bi, 0, 0))
    modspec = lambda k: pl.BlockSpec((None, bb, 1, D_MODEL), lambda bi, j: (l, mod_row0 + bi, 0, k))
    return pl.pallas_call(
        functools.partial(_ffn_kernel, bb=bb, ts=ts, prompt=prompt, final=final),
        grid=(b // bb, s // ts),
        in_specs=[tok, prev_spec, modspec(3), modspec(4), modspec(5), _layer_spec((1, D_MODEL), l),
                  _layer_spec((D_MODEL, 2 * D_FF), l), _layer_spec((CONV_W, 2 * D_FF), l),
                  _layer_spec((1, 2 * D_FF), l), _layer_spec((D_FF, D_MODEL), l), _const_spec((1, D_MODEL))],
        out_specs=[tok, pl.BlockSpec((bb, CONV_W - 1, 2 * D_FF), lambda bi, j: (bi, 0, 0))],
        out_shape=[jax.ShapeDtypeStruct((b, s, D_MODEL), F32),
                   jax.ShapeDtypeStruct((b, CONV_W - 1, 2 * D_FF), F32)],
        scratch_shapes=[pltpu.VMEM((bb, CONV_HALO + ts, max(FF_CHUNKS)), F32),
                        pltpu.VMEM((bb, CONV_HALO + ts, max(FF_CHUNKS)), F32)],
        compiler_params=_params("parallel", "arbitrary"),
        name=("ffn_prompt" if prompt else "ffn_sample") + ("_final" if final else ""),
    )(x, prev, mod, mod, mod, wts["g_ffn"], wts["w_up"], wts["w_conv"], wts["b_conv"], wts["w_down"], g_final)


def _rope_tables(pos):
    freq = ROPE_THETA ** (-jnp.arange(ROPE_HALF, dtype=F32) / ROPE_HALF)
    ang = pos[:, None] * freq[None, :]
    cos, sin = jnp.cos(ang), jnp.sin(ang)
    z16 = jnp.zeros_like(cos)
    one = jnp.ones((pos.shape[0], QK_NOPE), F32)
    zero = jnp.zeros((pos.shape[0], QK_NOPE), F32)
    span0, span1 = [cos, cos, z16, z16], [-sin, sin, z16, z16]
    cat = lambda parts: jnp.concatenate(parts, axis=-1)
    tab_k = jnp.stack([cat(span0 + span0), cat(span1 + span1)])
    tab_q = jnp.stack([cat([one] + span0), cat([zero] + span1)])
    return tab_k, tab_q * Q_PRESCALE


def _prep_weights(w_in, g_attn, g_q, w_uq, g_kv, w_uk, w_uv, w_pool, pool_scale, w_out, g_ffn, w_up, w_conv, b_conv,
                  w_down):
    depth = w_in.shape[0]
    o2 = IN_A
    o3 = o2 + QK_ROPE
    kr = w_in[:, :, o2:o3]
    span = [kr, kr[:, :, :ROPE_HALF], jnp.zeros((depth, D_MODEL, ROPE_HALF), F32)]
    w_in_p = jnp.concatenate([w_in[:, :, :o2]] + span + span + [w_in[:, :, o3:]], axis=2).astype(BF16)
    wq = w_uq.reshape(depth, Q_LORA, N_HEADS, QK_NOPE + QK_ROPE)
    w_q = jnp.concatenate([wq, wq[..., QK_NOPE:QK_NOPE + ROPE_HALF],
                           jnp.zeros((depth, Q_LORA, N_HEADS, ROPE_HALF), F32)], axis=-1)
    w_k = jnp.pad(w_uk, ((0, 0), (0, 0), (0, 0), (0, HEAD_BLOCK - QK_NOPE)))
    w_v = jnp.pad(w_uv, ((0, 0), (0, 0), (0, 0), (0, HEAD_BLOCK - V_HEAD)))
    w_ukt = jnp.pad(jnp.transpose(w_uk, (0, 2, 3, 1)), ((0, 0), (0, 0), (0, HEAD_BLOCK - QK_NOPE), (0, 0)))
    wv_heads = jnp.transpose(w_uv, (0, 2, 1, 3))
    zeros_v = jnp.zeros_like(wv_heads)
    even = (jnp.arange(N_HEADS) % HEADS_PER_STEP == 0)[None, :, None, None]
    w_vup = jnp.concatenate([jnp.where(even, wv_heads, zeros_v), jnp.where(even, zeros_v, wv_heads)], axis=-1)
    return dict(
        w_in=w_in_p, g_attn=g_attn[:, None], g_q=g_q[:, None], g_kv=g_kv[:, None],
        w_q=w_q.reshape(depth, Q_LORA, N_HEADS * HEAD_BLOCK).astype(BF16),
        w_k=w_k.reshape(depth, KV_LORA, N_HEADS * HEAD_BLOCK).astype(BF16),
        w_v=w_v.reshape(depth, KV_LORA, N_HEADS * HEAD_BLOCK).astype(BF16),
        w_ukt=w_ukt.astype(BF16), w_vup=w_vup.astype(BF16),
        w_pool=w_pool.astype(BF16), pool_scale=pool_scale[:, None], w_out=w_out.astype(BF16),
        g_ffn=g_ffn[:, None], w_up=w_up.astype(BF16), w_conv=w_conv,
        b_conv=b_conv[:, None], w_down=w_down.astype(BF16))


def _pick(n, pref):
    t = min(n, pref)
    while n % t:
        t -= 1
    return t


def kernel(x_prompt, x_sample, cache_latent, state_pool, state_conv, page_table, c_prompt, c_sample, w_ada, b_ada, g_attn, w_in, g_q, w_uq, g_kv, w_uk, w_uv, w_pool, pool_scale, w_out, g_ffn, w_up, w_conv, b_conv, w_down, g_final):
    depth = w_ada.shape[0]
    bp, sp, _ = x_prompt.shape
    bs, ss, _ = x_sample.shape
    past_len = page_table.shape[1] * PAGE_SIZE

    ts_p = _pick(sp, 512)
    bb_s = _pick(bs, 32)
    mod = _ada(jnp.concatenate([c_sample, c_prompt], axis=0), w_ada.astype(BF16), b_ada[:, None, :])
    mod = mod.reshape(depth, bs + bp, 1, -1)
    row0_s, row0_p = 0, bs
    tabk_p, tabq_p = _rope_tables(jnp.arange(sp, dtype=F32))
    tabk_s, tabq_s = _rope_tables(past_len + jnp.arange(ss, dtype=F32))
    gfin = g_final[None]
    cache_t = jnp.swapaxes(cache_latent, 2, 3)
    pool_t = jnp.swapaxes(state_pool, 1, 2)
    wts = _prep_weights(w_in, g_attn, g_q, w_uq, g_kv, w_uk, w_uv, w_pool, pool_scale, w_out, g_ffn, w_up, w_conv,
                        b_conv, w_down)

    yp, ys = x_prompt, x_sample
    outs = [[] for _ in range(6)]
    for l in range(depth):
        final = l == depth - 1

        cqn, lat, u, gate, kf, vf = _in_proj(yp.reshape(bp * sp, D_MODEL), mod, row0_p, tabk_p, wts, l,
                                             seq=sp, bb=1, ts=ts_p, prompt=True)
        q = _q_proj(cqn, tabq_p, wts["w_q"], l, seq=sp, bb=1, ts=ts_p, prompt=True)
        o = _attn(q.reshape(bp, sp, -1), kf.reshape(bp, sp, -1), vf.reshape(bp, sp, -1), tq=ts_p)
        u3 = u.reshape(bp, sp, D_MODEL)
        x1 = _mix(yp, u3, u3, gate.reshape(bp, sp, -1), o, mod, row0_p, wts, l,
                  bb=1, ts=ts_p, pos0=0, prompt=True)
        yp, conv_p = _ffn(x1, x1, mod, row0_p, wts, l, gfin, bb=1, ts=ts_p, prompt=True, final=final)
        outs[0].append(lat.reshape(bp, sp, LATENT))
        outs[1].append(u3[:, sp - POOL_CTX:, :])
        outs[2].append(conv_p)

        cqn, lat, u, gate = _in_proj(ys.reshape(bs * ss, D_MODEL), mod, row0_s, tabk_s, wts, l,
                                     seq=ss, bb=bb_s, ts=ss, prompt=False)
        q = _q_proj(cqn, tabq_s, wts["w_q"], l, seq=ss, bb=bb_s, ts=ss, prompt=False)
        qlat, qrope = _absorb(q, wts["w_ukt"], l, n_seq=bs, ts=ss)
        lat3 = lat.reshape(bs, ss, LATENT)
        olat = _paged_attn(page_table, qlat, qrope, lat3, cache_t, l)
        o = _v_up(olat, wts["w_vup"], l)
        u3 = u.reshape(bs, ss, D_MODEL)
        x1 = _mix(ys, u3, pool_t, gate.reshape(bs, ss, -1), o.reshape(bs, ss, -1), mod, row0_s, wts, l,
                  bb=bb_s, ts=ss, pos0=past_len, prompt=False)
        ys, conv_s = _ffn(x1, state_conv, mod, row0_s, wts, l, gfin, bb=bb_s, ts=ss, prompt=False, final=final)
        outs[3].append(lat3)
        outs[4].append(jnp.concatenate([state_pool[l], u3], axis=1)[:, -POOL_CTX:, :])
        outs[5].append(conv_s)

    return (yp, ys) + tuple(jnp.stack(o) for o in outs)
```

```python
import functools

import jax
import jax.numpy as jnp
from jax import lax
from jax.experimental import pallas as pl
from jax.experimental.pallas import tpu as pltpu

F32 = jnp.float32
BF16 = jnp.bfloat16

D_MODEL = 1024
N_HEADS = 16
QK_NOPE = 64
QK_ROPE = 32
ROPE_HALF = QK_ROPE // 2
V_HEAD = 64
KV_LORA = 256
Q_LORA = 768
LATENT = KV_LORA + QK_ROPE
ROPE_THETA = 10000.0
ATTN_SCALE = (QK_NOPE + QK_ROPE) ** -0.5
Q_PRESCALE = ATTN_SCALE * 1.4426950408889634
POOL_WINDOWS = (2, 4, 8, 16)
POOL_GW = D_MODEL // len(POOL_WINDOWS)
POOL_CTX = 15
D_FF = 2816
CONV_W = 3
PAGE_SIZE = 128
EPS = 1e-6
NEG_INF = -1e30

LANES = 128
SUBLANES = 8
HEAD_BLOCK = LANES
POOL_HALO = 16
IN_A = Q_LORA + KV_LORA
IN_KR = IN_A + LANES
IN_U = IN_KR + D_MODEL
IN_END = IN_U + 2 * D_MODEL
VMEM_LIMIT = 56 * 1024 * 1024

NT_DIMS = (((1,), (1,)), ((), ()))


def _dot(a, b):
    return jnp.dot(a, b, preferred_element_type=F32)


def _dot_nt(a, b):
    return lax.dot_general(a, b, NT_DIMS, preferred_element_type=F32)


def _rms(x, g):
    return x * lax.rsqrt(jnp.mean(x * x, axis=-1, keepdims=True) + EPS) * g


def _rows(ref, bb, ts):
    v = ref[...]
    n = v.shape[-1]
    if bb == 1:
        return v[0]
    return jnp.broadcast_to(v, (bb, ts, n)).reshape(bb * ts, n)


def _table_rows(tab_ref, idx, bb, ts):
    t = tab_ref[idx]
    if bb == 1:
        return t
    return jnp.broadcast_to(t[None], (bb, ts, LANES)).reshape(bb * ts, LANES)


def _rope_block(blk, t0, t1):
    return blk * t0 + pltpu.roll(blk, LANES - ROPE_HALF, 1) * t1


def _const_spec(shape):
    nd = len(shape)
    return pl.BlockSpec(shape, lambda *_: (0,) * nd, pipeline_mode=pl.Buffered(1))


def _layer_spec(shape, l):
    nd = len(shape)
    return pl.BlockSpec((None,) + tuple(shape), lambda *_: (l,) + (0,) * nd, pipeline_mode=pl.Buffered(1))


def _params(*sem):
    return pltpu.CompilerParams(dimension_semantics=sem, vmem_limit_bytes=VMEM_LIMIT)


def _ada_kernel(c_ref, w_ref, b_ref, o_ref):
    o_ref[...] = _dot(c_ref[...].astype(BF16), w_ref[...]) + b_ref[...]


def _ada(c_all, w_ada, b_ada):
    depth, d, n = w_ada.shape
    rows = c_all.shape[0]
    tn = D_MODEL
    return pl.pallas_call(
        _ada_kernel,
        grid=(depth, n // tn),
        in_specs=[pl.BlockSpec((rows, d), lambda l, j: (0, 0)),
                  pl.BlockSpec((None, d, tn), lambda l, j: (l, 0, j)),
                  pl.BlockSpec((None, 1, tn), lambda l, j: (l, 0, j))],
        out_specs=pl.BlockSpec((None, rows, tn), lambda l, j: (l, 0, j)),
        out_shape=jax.ShapeDtypeStruct((depth, rows, n), F32),
        compiler_params=_params("parallel", "parallel"),
        name="ada",
    )(c_all, w_ada, b_ada)


def _in_proj_kernel(*refs, bb, ts, prompt):
    if prompt:
        (x_ref, sh_ref, sc_ref, tab_ref, ga_ref, gq_ref, gkv_ref, win_ref, wk_ref, wv_ref,
         cqn_ref, lat_ref, u_ref, gate_ref, k_ref, v_ref) = refs
    else:
        (x_ref, sh_ref, sc_ref, tab_ref, ga_ref, gq_ref, gkv_ref, win_ref,
         cqn_ref, lat_ref, u_ref, gate_ref) = refs
    x = x_ref[...]
    h = _rms(x, ga_ref[...]) * (1.0 + _rows(sc_ref, bb, ts)) + _rows(sh_ref, bb, ts)
    hb = h.astype(BF16)
    a = _dot(hb, win_ref[:, 0:IN_A])
    cqn_ref[...] = _rms(a[:, 0:Q_LORA], gq_ref[...]).astype(BF16)
    lat = _rms(a[:, Q_LORA:IN_A], gkv_ref[...])
    lat_ref[:, 0:KV_LORA] = lat
    kr = _dot(hb, win_ref[:, IN_A:IN_KR])
    krr = _rope_block(kr, _table_rows(tab_ref, 0, bb, ts), _table_rows(tab_ref, 1, bb, ts))
    lat_ref[:, KV_LORA:LATENT] = krr[:, 0:QK_ROPE]
    u_ref[...] = _dot(hb, win_ref[:, IN_KR:IN_U])
    gate_ref[...] = jax.nn.sigmoid(_dot(hb, win_ref[:, IN_U:IN_END])).astype(gate_ref.dtype)
    if prompt:
        latb = lat.astype(BF16)
        lane = lax.broadcasted_iota(jnp.int32, krr.shape, 1)
        kr_head = jnp.where(lane >= QK_NOPE, krr, 0.0)
        kf = _dot(latb, wk_ref[...])
        vf = _dot(latb, wv_ref[...])
        for hd in range(N_HEADS):
            sl = slice(hd * HEAD_BLOCK, (hd + 1) * HEAD_BLOCK)
            k_ref[:, sl] = (kf[:, sl] + kr_head).astype(BF16)
            v_ref[:, sl] = jnp.where(lane == V_HEAD, 1.0, vf[:, sl]).astype(BF16)


def _in_proj(x, mod, mod_row0, tab, wts, l, *, seq, bb, ts, prompt):
    m = x.shape[0]
    tm = bb * ts
    nst = seq // ts
    row = lambda i: (i, 0)
    if prompt:
        modmap = lambda k: (lambda i: (l, mod_row0 + i // nst, 0, k))
        tabmap = lambda i: (0, i % nst, 0)
    else:
        modmap = lambda k: (lambda i: (l, mod_row0 + i, 0, k))
        tabmap = lambda i: (0, 0, 0)
    in_specs = [pl.BlockSpec((tm, D_MODEL), row),
                pl.BlockSpec((None, bb, 1, D_MODEL), modmap(0)),
                pl.BlockSpec((None, bb, 1, D_MODEL), modmap(1)),
                pl.BlockSpec((2, ts, LANES), tabmap),
                _layer_spec((1, D_MODEL), l), _layer_spec((1, Q_LORA), l), _layer_spec((1, KV_LORA), l),
                _layer_spec((D_MODEL, IN_END), l)]
    args = [x, mod, mod, tab, wts["g_attn"], wts["g_q"], wts["g_kv"], wts["w_in"]]
    out_specs = [pl.BlockSpec((tm, Q_LORA), row), pl.BlockSpec((tm, LATENT), row),
                 pl.BlockSpec((tm, D_MODEL), row), pl.BlockSpec((tm, 2 * D_MODEL), row)]
    out_shape = [jax.ShapeDtypeStruct((m, Q_LORA), BF16), jax.ShapeDtypeStruct((m, LATENT), F32),
                 jax.ShapeDtypeStruct((m, D_MODEL), F32),
                 jax.ShapeDtypeStruct((m, 2 * D_MODEL), BF16 if prompt else F32)]
    if prompt:
        in_specs += [_layer_spec((KV_LORA, N_HEADS * HEAD_BLOCK), l)] * 2
        args += [wts["w_k"], wts["w_v"]]
        out_specs += [pl.BlockSpec((tm, N_HEADS * HEAD_BLOCK), row)] * 2
        out_shape += [jax.ShapeDtypeStruct((m, N_HEADS * HEAD_BLOCK), BF16)] * 2
    return pl.pallas_call(
        functools.partial(_in_proj_kernel, bb=bb, ts=ts, prompt=prompt),
        grid=(m // tm,), in_specs=in_specs, out_specs=out_specs, out_shape=out_shape,
        compiler_params=_params("parallel"),
        name="in_proj_prompt" if prompt else "in_proj_sample",
    )(*args)


def _q_proj_kernel(cqn_ref, tab_ref, wq_ref, q_ref, *, bb, ts):
    q = _dot(cqn_ref[...], wq_ref[...])
    t0, t1 = (_table_rows(tab_ref, i, bb, ts) for i in range(2))
    for hd in range(N_HEADS):
        sl = slice(hd * HEAD_BLOCK, (hd + 1) * HEAD_BLOCK)
        q_ref[:, sl] = _rope_block(q[:, sl], t0, t1).astype(BF16)


def _q_proj(cqn, tab, w_q, l, *, seq, bb, ts, prompt):
    m = cqn.shape[0]
    tm = bb * ts
    nst = seq // ts
    tabmap = (lambda i: (0, i % nst, 0)) if prompt else (lambda i: (0, 0, 0))
    return pl.pallas_call(
        functools.partial(_q_proj_kernel, bb=bb, ts=ts),
        grid=(m // tm,),
        in_specs=[pl.BlockSpec((tm, Q_LORA), lambda i: (i, 0)),
                  pl.BlockSpec((2, ts, LANES), tabmap),
                  _layer_spec((Q_LORA, N_HEADS * HEAD_BLOCK), l)],
        out_specs=pl.BlockSpec((tm, N_HEADS * HEAD_BLOCK), lambda i: (i, 0)),
        out_shape=jax.ShapeDtypeStruct((m, N_HEADS * HEAD_BLOCK), BF16),
        compiler_params=_params("parallel"),
        name="q_proj_prompt" if prompt else "q_proj_sample",
    )(cqn, tab, w_q)


HEADS_PER_STEP = 2


ATTN_STRIP = 16


def _attn_groups(nq):
    if nq % 2:
        return [(g,) for g in range(nq)]
    return [(g, nq - 1 - g) for g in range(nq // 2)]


def _attn_kernel(q_ref, k_ref, v_ref, o_ref, s_ref, p_ref, m_ref, alpha_ref, acc_ref, *, tq, groups):
    out0 = {}

    def run_group(group):
        tiles = [(slot, qi, j) for slot, qi in enumerate(group) for j in range(qi + 1)]
        for t, (slot, qi, j) in enumerate(tiles):
            buf = t % 2
            first, last = j == 0, j == qi
            halves = 2 if last else 1
            hrows = tq // halves
            spans = [(slice(h * hrows, (h + 1) * hrows), (h + 1) * hrows) for h in range(halves)]
            for hh in range(HEADS_PER_STEP):
                hs = slice(hh * HEAD_BLOCK, (hh + 1) * HEAD_BLOCK)
                for rs, kw in spans:
                    s_ref[buf, hh, rs, 0:kw] = _dot_nt(q_ref[0, qi * tq + rs.start:qi * tq + rs.stop, hs],
                                                       k_ref[0, j * tq:j * tq + kw, hs])
            for hh in range(HEADS_PER_STEP):
                hs = slice(hh * HEAD_BLOCK, (hh + 1) * HEAD_BLOCK)
                for r in range(tq // ATTN_STRIP):
                    rows = slice(r * ATTN_STRIP, (r + 1) * ATTN_STRIP)
                    kw = spans[r * ATTN_STRIP // hrows][1]
                    s = s_ref[buf, hh, rows, 0:kw]
                    if last:
                        qpos = r * ATTN_STRIP + lax.broadcasted_iota(jnp.int32, (ATTN_STRIP, kw), 0)
                        kpos = lax.broadcasted_iota(jnp.int32, (ATTN_STRIP, kw), 1)
                        s = jnp.where(kpos <= qpos, s, NEG_INF)
                    mx = jnp.max(s, axis=-1, keepdims=True)
                    if first:
                        m_new = jnp.broadcast_to(mx, (ATTN_STRIP, LANES))
                    else:
                        m_prev = m_ref[slot, hh, rows, :]
                        m_new = jnp.maximum(m_prev, mx)
                        alpha_ref[buf, hh, rows, :] = jnp.exp2(m_prev - m_new)
                    if not last:
                        m_ref[slot, hh, rows, :] = m_new
                    p_ref[buf, hh, rows, 0:kw] = jnp.exp2(s - jnp.tile(m_new, (1, kw // LANES))).astype(BF16)
                for rs, kw in spans:
                    pv = _dot(p_ref[buf, hh, rs, 0:kw], v_ref[0, j * tq:j * tq + kw, hs])
                    acc = pv if first else alpha_ref[buf, hh, rs, :] * acc_ref[slot, hh, rs, :] + pv
                    if last:
                        out = acc / acc[:, V_HEAD:V_HEAD + 1]
                        if hh == 0:
                            out0[rs.start] = out
                        else:
                            lane = lax.broadcasted_iota(jnp.int32, out.shape, 1)
                            both = jnp.where(lane < V_HEAD, out0[rs.start], pltpu.roll(out, V_HEAD, 1))
                            o_ref[0, qi * tq + rs.start:qi * tq + rs.stop, :] = both.astype(BF16)
                    else:
                        acc_ref[slot, hh, rs, :] = acc

    if len(groups) == 1:
        run_group(groups[0])
    else:
        for g, group in enumerate(groups):
            pl.when(pl.program_id(2) == g)(functools.partial(run_group, group))


def _attn(q, k, v, *, tq):
    b, s, _ = q.shape
    hw = HEADS_PER_STEP * HEAD_BLOCK
    vw = HEADS_PER_STEP * V_HEAD
    groups = _attn_groups(s // tq)
    slots = len(groups[0])
    seq = pl.BlockSpec((1, s, hw), lambda bi, hp, g: (bi, 0, hp))
    return pl.pallas_call(
        functools.partial(_attn_kernel, tq=tq, groups=groups),
        grid=(b, N_HEADS // HEADS_PER_STEP, len(groups)),
        in_specs=[seq, seq, seq],
        out_specs=pl.BlockSpec((1, s, vw), lambda bi, hp, g: (bi, 0, hp)),
        out_shape=jax.ShapeDtypeStruct((b, s, N_HEADS * V_HEAD), BF16),
        scratch_shapes=[pltpu.VMEM((2, HEADS_PER_STEP, tq, tq), F32), pltpu.VMEM((2, HEADS_PER_STEP, tq, tq), BF16),
                        pltpu.VMEM((slots, HEADS_PER_STEP, tq, LANES), F32),
                        pltpu.VMEM((2, HEADS_PER_STEP, tq, LANES), F32),
                        pltpu.VMEM((slots, HEADS_PER_STEP, tq, HEAD_BLOCK), F32)],
        compiler_params=_params("parallel", "parallel", "arbitrary"),
        name="attn_prompt",
    )(q, k, v)


def _absorb_kernel(q_ref, w_ref, qlat_ref, qrope_ref, *, n_seq, ts):
    blk = q_ref[...]
    qlat = _dot(blk, w_ref[0])
    qlat_ref[...] = qlat.reshape(n_seq, ts, KV_LORA)
    rolled = pltpu.roll(blk.astype(F32), LANES - QK_NOPE, 1)
    qrope_ref[...] = rolled[:, 0:QK_ROPE].reshape(n_seq, ts, QK_ROPE)


def _absorb(q, w_ukt, l, *, n_seq, ts):
    m = q.shape[0]
    return pl.pallas_call(
        functools.partial(_absorb_kernel, n_seq=n_seq, ts=ts),
        grid=(N_HEADS,),
        in_specs=[pl.BlockSpec((m, HEAD_BLOCK), lambda h: (0, h)),
                  pl.BlockSpec((None, 1, HEAD_BLOCK, KV_LORA), lambda h: (l, h, 0, 0))],
        out_specs=[pl.BlockSpec((n_seq, None, ts, KV_LORA), lambda h: (0, h, 0, 0)),
                   pl.BlockSpec((n_seq, None, ts, QK_ROPE), lambda h: (0, h, 0, 0))],
        out_shape=[jax.ShapeDtypeStruct((n_seq, N_HEADS, ts, KV_LORA), F32),
                   jax.ShapeDtypeStruct((n_seq, N_HEADS, ts, QK_ROPE), F32)],
        compiler_params=_params("parallel"),
        name="absorb",
    )(q, w_ukt)


PAGES_PER_CHUNK = 16


def _paged_attn_kernel(pt_ref, qlat_ref, qrope_ref, new_ref, cache_ref, o_ref, buf_ref, kt_ref, s_ref, sem,
                       *, n_pages, ts, layer):
    b = pl.program_id(0)
    nq = N_HEADS * ts
    slot = b % 2

    def page_copy(seq, j, slot_):
        page = pt_ref[seq * n_pages + j]
        return pltpu.make_async_copy(cache_ref.at[layer, page], buf_ref.at[slot_, j], sem.at[slot_])

    @pl.when(b == 0)
    def _():
        for j in range(n_pages):
            page_copy(0, j, 0).start()

    @pl.when(b + 1 < pl.num_programs(0))
    def _():
        for j in range(n_pages):
            page_copy(b + 1, j, 1 - slot).start()

    for j in range(n_pages):
        page_copy(b, j, slot).wait()
    ql = qlat_ref[0].reshape(nq, KV_LORA).astype(BF16)
    qr = qrope_ref[0].reshape(nq, QK_ROPE).astype(BF16)
    new = new_ref[0]
    nl = new[:, 0:KV_LORA].astype(BF16)
    s_new = _dot_nt(ql, nl) + _dot_nt(qr, new[:, KV_LORA:LATENT].astype(BF16))
    tok = lax.broadcasted_iota(jnp.int32, (nq, ts), 0) % ts
    key = lax.broadcasted_iota(jnp.int32, (nq, ts), 1)
    s_new = jnp.where(key <= tok, s_new, NEG_INF)
    m = jnp.max(s_new, axis=-1, keepdims=True)
    chunk_pages = min(n_pages, PAGES_PER_CHUNK)
    chunks = [slice(c * chunk_pages * PAGE_SIZE, (c + 1) * chunk_pages * PAGE_SIZE)
              for c in range(n_pages // chunk_pages)]
    for c, cols in enumerate(chunks):
        for j in range(c * chunk_pages, (c + 1) * chunk_pages):
            kt_ref[:, j * PAGE_SIZE:(j + 1) * PAGE_SIZE] = buf_ref[slot, j].astype(BF16)
        s = _dot(ql, kt_ref[0:KV_LORA, cols]) + _dot(qr, kt_ref[KV_LORA:LATENT, cols])
        s_ref[:, cols] = s
        m = jnp.maximum(m, jnp.max(s, axis=-1, keepdims=True))
    p_new = jnp.exp2(s_new - m)
    l = jnp.sum(p_new, axis=-1, keepdims=True)
    acc = _dot(p_new.astype(BF16), nl)
    for cols in chunks:
        p = jnp.exp2(s_ref[:, cols] - m)
        l = l + jnp.sum(p, axis=-1, keepdims=True)
        acc = acc + _dot_nt(p.astype(BF16), kt_ref[0:KV_LORA, cols])
    o = acc / l
    o_ref[0] = o.reshape(N_HEADS, ts, KV_LORA)


def _paged_attn(page_table, qlat, qrope, lat_new, cache_t, layer):
    n_seq, n_pages = page_table.shape
    ts = qlat.shape[2]

    grid_spec = pltpu.PrefetchScalarGridSpec(
        num_scalar_prefetch=1,
        grid=(n_seq,),
        in_specs=[pl.BlockSpec((1, N_HEADS, ts, KV_LORA), lambda b, pt: (b, 0, 0, 0)),
                  pl.BlockSpec((1, N_HEADS, ts, QK_ROPE), lambda b, pt: (b, 0, 0, 0)),
                  pl.BlockSpec((1, ts, LATENT), lambda b, pt: (b, 0, 0)),
                  pl.BlockSpec(memory_space=pltpu.HBM)],
        out_specs=pl.BlockSpec((1, N_HEADS, ts, KV_LORA), lambda b, pt: (b, 0, 0, 0)),
        scratch_shapes=[pltpu.VMEM((2, n_pages, LATENT, PAGE_SIZE), F32),
                        pltpu.VMEM((LATENT, n_pages * PAGE_SIZE), BF16),
                        pltpu.VMEM((N_HEADS * ts, n_pages * PAGE_SIZE), F32),
                        pltpu.SemaphoreType.DMA((2,))],
    )
    return pl.pallas_call(
        functools.partial(_paged_attn_kernel, n_pages=n_pages, ts=ts, layer=layer),
        grid_spec=grid_spec,
        out_shape=jax.ShapeDtypeStruct((n_seq, N_HEADS, ts, KV_LORA), F32),
        compiler_params=_params("arbitrary"),
        name="paged_attn",
    )(page_table.reshape(-1), qlat, qrope, lat_new, cache_t)


def _v_up_kernel(o_ref, w_ref, out_ref, *, n_seq, ts):
    acc = None
    for e in range(HEADS_PER_STEP):
        x = o_ref[:, e].reshape(n_seq * ts, KV_LORA).astype(BF16)
        d = _dot(x, w_ref[e])
        acc = d if acc is None else acc + d
    out_ref[...] = acc


def _v_up(olat, w_vup, l):
    n_seq, _, ts, _ = olat.shape
    vw = HEADS_PER_STEP * V_HEAD
    return pl.pallas_call(
        functools.partial(_v_up_kernel, n_seq=n_seq, ts=ts),
        grid=(N_HEADS // HEADS_PER_STEP,),
        in_specs=[pl.BlockSpec((n_seq, HEADS_PER_STEP, ts, KV_LORA), lambda p: (0, p, 0, 0)),
                  pl.BlockSpec((None, HEADS_PER_STEP, KV_LORA, vw), lambda p: (l, p, 0, 0))],
        out_specs=pl.BlockSpec((n_seq * ts, vw), lambda p: (0, p)),
        out_shape=jax.ShapeDtypeStruct((n_seq * ts, N_HEADS * V_HEAD), F32),
        compiler_params=_params("parallel"),
        name="v_up",
    )(olat, w_vup)


def _mix_kernel(x_ref, u_ref, prev_ref, gate_ref, o_ref, g1_ref, wpool_ref, pscale_ref, wout_ref,
                x1_ref, ext_ref, *, bb, ts, pos0, prompt):
    j = pl.program_id(1)
    tm = bb * ts
    u = u_ref[...]
    ext_ref[:, POOL_HALO:, :] = u
    if prompt:
        ext_ref[:, 0:POOL_HALO, :] = jnp.where(j > 0, prev_ref[...], 0.0)
    else:
        for t in range(POOL_CTX):
            ext_ref[:, POOL_HALO - POOL_CTX + t, :] = prev_ref[t]
    pos = pos0 + j * ts + lax.broadcasted_iota(jnp.int32, (1, ts, 1), 1)
    mixed = []
    for g, w in enumerate(POOL_WINDOWS):
        cs = slice(g * POOL_GW, (g + 1) * POOL_GW)
        acc = ext_ref[:, POOL_HALO:POOL_HALO + ts, cs]
        for t in range(1, w):
            acc = acc + ext_ref[:, POOL_HALO - t:POOL_HALO - t + ts, cs]
        cnt = jnp.minimum(pos + 1, w).astype(F32)
        y = acc / cnt - u[:, :, cs]
        z = _dot(y.reshape(tm, POOL_GW).astype(BF16), wpool_ref[g]) * pscale_ref[:, cs]
        ga = gate_ref[:, :, cs].astype(F32).reshape(tm, POOL_GW)
        gb = gate_ref[:, :, D_MODEL + g * POOL_GW:D_MODEL + (g + 1) * POOL_GW].astype(F32).reshape(tm, POOL_GW)
        mixed.append((ga * z + gb * o_ref[:, :, cs].astype(F32).reshape(tm, POOL_GW)).astype(BF16))
    out = _dot(jnp.concatenate(mixed, axis=-1), wout_ref[...])
    x1 = x_ref[...].reshape(tm, D_MODEL) + _rows(g1_ref, bb, ts) * out
    x1_ref[...] = x1.reshape(bb, ts, D_MODEL)


def _mix(x, u, prev, gate, o, mod, mod_row0, wts, l, *, bb, ts, pos0, prompt):
    b, s, _ = x.shape
    tok = lambda n: pl.BlockSpec((bb, ts, n), lambda bi, j: (bi, j, 0))
    if prompt:
        r = ts // POOL_HALO
        prev_spec = pl.BlockSpec((bb, POOL_HALO, D_MODEL), lambda bi, j: (bi, jnp.maximum(j * r - 1, 0), 0))
    else:
        prev_spec = pl.BlockSpec((None, POOL_CTX, bb, D_MODEL), lambda bi, j: (l, 0, bi, 0))
    return pl.pallas_call(
        functools.partial(_mix_kernel, bb=bb, ts=ts, pos0=pos0, prompt=prompt),
        grid=(b // bb, s // ts),
        in_specs=[tok(D_MODEL), tok(D_MODEL), prev_spec, tok(2 * D_MODEL), tok(D_MODEL),
                  pl.BlockSpec((None, bb, 1, D_MODEL), lambda bi, j: (l, mod_row0 + bi, 0, 2)),
                  _layer_spec((len(POOL_WINDOWS), POOL_GW, POOL_GW), l), _layer_spec((1, D_MODEL), l),
                  _layer_spec((D_MODEL, D_MODEL), l)],
        out_specs=tok(D_MODEL),
        out_shape=jax.ShapeDtypeStruct((b, s, D_MODEL), F32),
        scratch_shapes=[pltpu.VMEM((bb, POOL_HALO + ts, D_MODEL), F32)],
        compiler_params=_params("parallel", "arbitrary"),
        name="mix_prompt" if prompt else "mix_sample",
    )(x, u, prev, gate, o, mod, wts["w_pool"], wts["pool_scale"], wts["w_out"])


MXU_DIM = 256
FF_CHUNKS = (6 * MXU_DIM, 5 * MXU_DIM)
FF_STARTS = tuple(sum(FF_CHUNKS[:c]) for c in range(len(FF_CHUNKS)))
assert sum(FF_CHUNKS) == D_FF
CONV_HALO = SUBLANES


def _ffn_kernel(x_ref, prev_ref, sh_ref, sc_ref, g2_ref, gf_ref, wup_ref, wconv_ref, bconv_ref, wdown_ref,
                gfin_ref, y_ref, cnew_ref, extg_ref, extv_ref, *, bb, ts, prompt, final):
    j = pl.program_id(1)
    tm = bb * ts
    x = x_ref[...].reshape(tm, D_MODEL)
    sc = _rows(sc_ref, bb, ts)
    sh = _rows(sh_ref, bb, ts)
    if prompt:
        xe = jnp.concatenate([prev_ref[0], x], axis=0)
    else:
        xe = x
    hb = (_rms(xe, gf_ref[...]) * (1.0 + sc) + sh).astype(BF16)
    acc = jnp.zeros((tm, D_MODEL), F32)
    for start, width in zip(FF_STARTS, FF_CHUNKS):
        conv = []
        for part, ext_ref in ((0, extg_ref), (1, extv_ref)):
            cs = slice(part * D_FF + start, part * D_FF + start + width)
            z = _dot(hb, wup_ref[:, cs])
            if prompt:
                keep = jnp.logical_or(j > 0, lax.broadcasted_iota(jnp.int32, (tm + CONV_HALO, 1), 0) >= CONV_HALO)
                ext_ref[0, :, 0:width] = jnp.where(keep, z, 0.0)
            else:
                ext_ref[:, CONV_HALO:, 0:width] = z.reshape(bb, ts, width)
                ext_ref[:, CONV_HALO - (CONV_W - 1):CONV_HALO, 0:width] = prev_ref[:, :, cs]
            out = bconv_ref[:, cs][None]
            for t in range(CONV_W):
                lo_r = CONV_HALO - (CONV_W - 1) + t
                out = out + ext_ref[:, lo_r:lo_r + ts, 0:width] * wconv_ref[t:t + 1, cs][None]
            cnew_ref[:, :, cs] = ext_ref[:, CONV_HALO + ts - (CONV_W - 1):CONV_HALO + ts, 0:width]
            conv.append(out.reshape(tm, width))
        act = (jax.nn.silu(conv[0]) * conv[1]).astype(BF16)
        acc = acc + _dot(act, wdown_ref[start:start + width, :])
    x2 = x + _rows(g2_ref, bb, ts) * acc
    if final:
        x2 = _rms(x2, gfin_ref[...])
    y_ref[...] = x2.reshape(bb, ts, D_MODEL)


def _ffn(x, prev, mod, mod_row0, wts, l, g_final, *, bb, ts, prompt, final):
    b, s, _ = x.shape
    tok = pl.BlockSpec((bb, ts, D_MODEL), lambda bi, j: (bi, j, 0))
    if prompt:
        r = ts // CONV_HALO
        prev_spec = pl.BlockSpec((bb, CONV_HALO, D_MODEL), lambda bi, j: (bi, jnp.maximum(j * r - 1, 0), 0))
    else:
        prev_spec = pl.BlockSpec((None, bb, CONV_W - 1, 2 * D_FF), lambda bi, j: (l, bi, 0, 0))
    modspec = lambda k: pl.BlockSpec((None, bb, 1, D_MODEL), lambda bi, j: (l, mod_row0 + bi, 0, k))
    return pl.pallas_call(
        functools.partial(_ffn_kernel, bb=bb, ts=ts, prompt=prompt, final=final),
        grid=(b // bb, s // ts),
        in_specs=[tok, prev_spec, modspec(3), modspec(4), modspec(5), _layer_spec((1, D_MODEL), l),
                  _layer_spec((D_MODEL, 2 * D_FF), l), _layer_spec((CONV_W, 2 * D_FF), l),
                  _layer_spec((1, 2 * D_FF), l), _layer_spec((D_FF, D_MODEL), l), _const_spec((1, D_MODEL))],
        out_specs=[tok, pl.BlockSpec((bb, CONV_W - 1, 2 * D_FF), lambda bi, j: (bi, 0, 0))],
        out_shape=[jax.ShapeDtypeStruct((b, s, D_MODEL), F32),
                   jax.ShapeDtypeStruct((b, CONV_W - 1, 2 * D_FF), F32)],
        scratch_shapes=[pltpu.VMEM((bb, CONV_HALO + ts, max(FF_CHUNKS)), F32),
                        pltpu.VMEM((bb, CONV_HALO + ts, max(FF_CHUNKS)), F32)],
        compiler_params=_params("parallel", "arbitrary"),
        name=("ffn_prompt" if prompt else "ffn_sample") + ("_final" if final else ""),
    )(x, prev, mod, mod, mod, wts["g_ffn"], wts["w_up"], wts["w_conv"], wts["b_conv"], wts["w_down"], g_final)


def _rope_tables(pos):
    freq = ROPE_THETA ** (-jnp.arange(ROPE_HALF, dtype=F32) / ROPE_HALF)
    ang = pos[:, None] * freq[None, :]
    cos, sin = jnp.cos(ang), jnp.sin(ang)
    z16 = jnp.zeros_like(cos)
    one = jnp.ones((pos.shape[0], QK_NOPE), F32)
    zero = jnp.zeros((pos.shape[0], QK_NOPE), F32)
    span0, span1 = [cos, cos, z16, z16], [-sin, sin, z16, z16]
    cat = lambda parts: jnp.concatenate(parts, axis=-1)
    tab_k = jnp.stack([cat(span0 + span0), cat(span1 + span1)])
    tab_q = jnp.stack([cat([one] + span0), cat([zero] + span1)])
    return tab_k, tab_q * Q_PRESCALE


def _prep_weights(w_in, g_attn, g_q, w_uq, g_kv, w_uk, w_uv, w_pool, pool_scale, w_out, g_ffn, w_up, w_conv, b_conv,
                  w_down):
    depth = w_in.shape[0]
    o2 = IN_A
    o3 = o2 + QK_ROPE
    kr = w_in[:, :, o2:o3]
    span = [kr, kr[:, :, :ROPE_HALF], jnp.zeros((depth, D_MODEL, ROPE_HALF), F32)]
    w_in_p = jnp.concatenate([w_in[:, :, :o2]] + span + span + [w_in[:, :, o3:]], axis=2).astype(BF16)
    wq = w_uq.reshape(depth, Q_LORA, N_HEADS, QK_NOPE + QK_ROPE)
    w_q = jnp.concatenate([wq, wq[..., QK_NOPE:QK_NOPE + ROPE_HALF],
                           jnp.zeros((depth, Q_LORA, N_HEADS, ROPE_HALF), F32)], axis=-1)
    w_k = jnp.pad(w_uk, ((0, 0), (0, 0), (0, 0), (0, HEAD_BLOCK - QK_NOPE)))
    w_v = jnp.pad(w_uv, ((0, 0), (0, 0), (0, 0), (0, HEAD_BLOCK - V_HEAD)))
    w_ukt = jnp.pad(jnp.transpose(w_uk, (0, 2, 3, 1)), ((0, 0), (0, 0), (0, HEAD_BLOCK - QK_NOPE), (0, 0)))
    wv_heads = jnp.transpose(w_uv, (0, 2, 1, 3))
    zeros_v = jnp.zeros_like(wv_heads)
    even = (jnp.arange(N_HEADS) % HEADS_PER_STEP == 0)[None, :, None, None]
    w_vup = jnp.concatenate([jnp.where(even, wv_heads, zeros_v), jnp.where(even, zeros_v, wv_heads)], axis=-1)
    return dict(
        w_in=w_in_p, g_attn=g_attn[:, None], g_q=g_q[:, None], g_kv=g_kv[:, None],
        w_q=w_q.reshape(depth, Q_LORA, N_HEADS * HEAD_BLOCK).astype(BF16),
        w_k=w_k.reshape(depth, KV_LORA, N_HEADS * HEAD_BLOCK).astype(BF16),
        w_v=w_v.reshape(depth, KV_LORA, N_HEADS * HEAD_BLOCK).astype(BF16),
        w_ukt=w_ukt.astype(BF16), w_vup=w_vup.astype(BF16),
        w_pool=w_pool.astype(BF16), pool_scale=pool_scale[:, None], w_out=w_out.astype(BF16),
        g_ffn=g_ffn[:, None], w_up=w_up.astype(BF16), w_conv=w_conv,
        b_conv=b_conv[:, None], w_down=w_down.astype(BF16))


def _pick(n, pref):
    t = min(n, pref)
    while n % t:
        t -= 1
    return t


def kernel(x_prompt, x_sample, cache_latent, state_pool, state_conv, page_table, c_prompt, c_sample, w_ada, b_ada, g_attn, w_in, g_q, w_uq, g_kv, w_uk, w_uv, w_pool, pool_scale, w_out, g_ffn, w_up, w_conv, b_conv, w_down, g_final):
    depth = w_ada.shape[0]
    bp, sp, _ = x_prompt.shape
    bs, ss, _ = x_sample.shape
    past_len = page_table.shape[1] * PAGE_SIZE

    ts_p = _pick(sp, 512)
    bb_s = _pick(bs, 32)
    mod = _ada(jnp.concatenate([c_sample, c_prompt], axis=0), w_ada.astype(BF16), b_ada[:, None, :])
    mod = mod.reshape(depth, bs + bp, 1, -1)
    row0_s, row0_p = 0, bs
    tabk_p, tabq_p = _rope_tables(jnp.arange(sp, dtype=F32))
    tabk_s, tabq_s = _rope_tables(past_len + jnp.arange(ss, dtype=F32))
    gfin = g_final[None]
    cache_t = jnp.swapaxes(cache_latent, 2, 3)
    pool_t = jnp.swapaxes(state_pool, 1, 2)
    wts = _prep_weights(w_in, g_attn, g_q, w_uq, g_kv, w_uk, w_uv, w_pool, pool_scale, w_out, g_ffn, w_up, w_conv,
                        b_conv, w_down)

    yp, ys = x_prompt, x_sample
    outs = [[] for _ in range(6)]
    for l in range(depth):
        final = l == depth - 1

        cqn, lat, u, gate, kf, vf = _in_proj(yp.reshape(bp * sp, D_MODEL), mod, row0_p, tabk_p, wts, l,
                                             seq=sp, bb=1, ts=ts_p, prompt=True)
        q = _q_proj(cqn, tabq_p, wts["w_q"], l, seq=sp, bb=1, ts=ts_p, prompt=True)
        o = _attn(q.reshape(bp, sp, -1), kf.reshape(bp, sp, -1), vf.reshape(bp, sp, -1), tq=ts_p)
        u3 = u.reshape(bp, sp, D_MODEL)
        x1 = _mix(yp, u3, u3, gate.reshape(bp, sp, -1), o, mod, row0_p, wts, l,
                  bb=1, ts=ts_p, pos0=0, prompt=True)
        yp, conv_p = _ffn(x1, x1, mod, row0_p, wts, l, gfin, bb=1, ts=ts_p, prompt=True, final=final)
        outs[0].append(lat.reshape(bp, sp, LATENT))
        outs[1].append(u3[:, sp - POOL_CTX:, :])
        outs[2].append(conv_p)

        cqn, lat, u, gate = _in_proj(ys.reshape(bs * ss, D_MODEL), mod, row0_s, tabk_s, wts, l,
                                     seq=ss, bb=bb_s, ts=ss, prompt=False)
        q = _q_proj(cqn, tabq_s, wts["w_q"], l, seq=ss, bb=bb_s, ts=ss, prompt=False)
        qlat, qrope = _absorb(q, wts["w_ukt"], l, n_seq=bs, ts=ss)
        lat3 = lat.reshape(bs, ss, LATENT)
        olat = _paged_attn(page_table, qlat, qrope, lat3, cache_t, l)
        o = _v_up(olat, wts["w_vup"], l)
        u3 = u.reshape(bs, ss, D_MODEL)
        x1 = _mix(ys, u3, pool_t, gate.reshape(bs, ss, -1), o.reshape(bs, ss, -1), mod, row0_s, wts, l,
                  bb=bb_s, ts=ss, pos0=past_len, prompt=False)
        ys, conv_s = _ffn(x1, state_conv, mod, row0_s, wts, l, gfin, bb=bb_s, ts=ss, prompt=False, final=final)
        outs[3].append(lat3)
        outs[4].append(jnp.concatenate([state_pool[l], u3], axis=1)[:, -POOL_CTX:, :])
        outs[5].append(conv_s)

    return (yp, ys) + tuple(jnp.stack(o) for o in outs)
```

```python
import functools

import jax
import jax.numpy as jnp
from jax import lax
from jax.experimental import pallas as pl
from jax.experimental.pallas import tpu as pltpu

F32 = jnp.float32
BF16 = jnp.bfloat16

D_MODEL = 1024
N_HEADS = 16
QK_NOPE = 64
QK_ROPE = 32
ROPE_HALF = QK_ROPE // 2
V_HEAD = 64
KV_LORA = 256
Q_LORA = 768
LATENT = KV_LORA + QK_ROPE
ROPE_THETA = 10000.0
ATTN_SCALE = (QK_NOPE + QK_ROPE) ** -0.5
Q_PRESCALE = ATTN_SCALE * 1.4426950408889634
POOL_WINDOWS = (2, 4, 8, 16)
POOL_GW = D_MODEL // len(POOL_WINDOWS)
POOL_CTX = 15
D_FF = 2816
CONV_W = 3
PAGE_SIZE = 128
EPS = 1e-6
NEG_INF = -1e30

LANES = 128
SUBLANES = 8
HEAD_BLOCK = LANES
POOL_HALO = 16
IN_A = Q_LORA + KV_LORA
IN_KR = IN_A + LANES
IN_U = IN_KR + D_MODEL
IN_END = IN_U + 2 * D_MODEL
VMEM_LIMIT = 56 * 1024 * 1024

NT_DIMS = (((1,), (1,)), ((), ()))


def _dot(a, b):
    return jnp.dot(a, b, preferred_element_type=F32)


def _dot_nt(a, b):
    return lax.dot_general(a, b, NT_DIMS, preferred_element_type=F32)


def _rms(x, g):
    return x * lax.rsqrt(jnp.mean(x * x, axis=-1, keepdims=True) + EPS) * g


def _rows(ref, bb, ts):
    v = ref[...]
    n = v.shape[-1]
    if bb == 1:
        return v[0]
    return jnp.broadcast_to(v, (bb, ts, n)).reshape(bb * ts, n)


def _table_rows(tab_ref, idx, bb, ts):
    t = tab_ref[idx]
    if bb == 1:
        return t
    return jnp.broadcast_to(t[None], (bb, ts, LANES)).reshape(bb * ts, LANES)


def _rope_block(blk, t0, t1):
    return blk * t0 + pltpu.roll(blk, LANES - ROPE_HALF, 1) * t1


def _const_spec(shape):
    nd = len(shape)
    return pl.BlockSpec(shape, lambda *_: (0,) * nd, pipeline_mode=pl.Buffered(1))


def _layer_spec(shape, l):
    nd = len(shape)
    return pl.BlockSpec((None,) + tuple(shape), lambda *_: (l,) + (0,) * nd, pipeline_mode=pl.Buffered(1))


def _params(*sem):
    return pltpu.CompilerParams(dimension_semantics=sem, vmem_limit_bytes=VMEM_LIMIT)


def _ada_kernel(c_ref, w_ref, b_ref, o_ref):
    o_ref[...] = _dot(c_ref[...].astype(BF16), w_ref[...]) + b_ref[...]


def _ada(c_all, w_ada, b_ada):
    depth, d, n = w_ada.shape
    rows = c_all.shape[0]
    tn = D_MODEL
    return pl.pallas_call(
        _ada_kernel,
        grid=(depth, n // tn),
        in_specs=[pl.BlockSpec((rows, d), lambda l, j: (0, 0)),
                  pl.BlockSpec((None, d, tn), lambda l, j: (l, 0, j)),
                  pl.BlockSpec((None, 1, tn), lambda l, j: (l, 0, j))],
        out_specs=pl.BlockSpec((None, rows, tn), lambda l, j: (l, 0, j)),
        out_shape=jax.ShapeDtypeStruct((depth, rows, n), F32),
        compiler_params=_params("parallel", "parallel"),
        name="ada",
    )(c_all, w_ada, b_ada)


def _in_proj_kernel(*refs, bb, ts, prompt):
    if prompt:
        (x_ref, sh_ref, sc_ref, tab_ref, ga_ref, gq_ref, gkv_ref, win_ref, wk_ref, wv_ref,
         cqn_ref, lat_ref, u_ref, gate_ref, k_ref, v_ref) = refs
    else:
        (x_ref, sh_ref, sc_ref, tab_ref, ga_ref, gq_ref, gkv_ref, win_ref,
         cqn_ref, lat_ref, u_ref, gate_ref) = refs
    x = x_ref[...]
    h = _rms(x, ga_ref[...]) * (1.0 + _rows(sc_ref, bb, ts)) + _rows(sh_ref, bb, ts)
    hb = h.astype(BF16)
    a = _dot(hb, win_ref[:, 0:IN_A])
    cqn_ref[...] = _rms(a[:, 0:Q_LORA], gq_ref[...]).astype(BF16)
    lat = _rms(a[:, Q_LORA:IN_A], gkv_ref[...])
    lat_ref[:, 0:KV_LORA] = lat
    kr = _dot(hb, win_ref[:, IN_A:IN_KR])
    krr = _rope_block(kr, _table_rows(tab_ref, 0, bb, ts), _table_rows(tab_ref, 1, bb, ts))
    lat_ref[:, KV_LORA:LATENT] = krr[:, 0:QK_ROPE]
    u_ref[...] = _dot(hb, win_ref[:, IN_KR:IN_U])
    gate_ref[...] = jax.nn.sigmoid(_dot(hb, win_ref[:, IN_U:IN_END])).astype(gate_ref.dtype)
    if prompt:
        latb = lat.astype(BF16)
        lane = lax.broadcasted_iota(jnp.int32, krr.shape, 1)
        kr_head = jnp.where(lane >= QK_NOPE, krr, 0.0)
        kf = _dot(latb, wk_ref[...])
        vf = _dot(latb, wv_ref[...])
        for hd in range(N_HEADS):
            sl = slice(hd * HEAD_BLOCK, (hd + 1) * HEAD_BLOCK)
            k_ref[:, sl] = (kf[:, sl] + kr_head).astype(BF16)
            v_ref[:, sl] = jnp.where(lane == V_HEAD, 1.0, vf[:, sl]).astype(BF16)


def _in_proj(x, mod, mod_row0, tab, wts, l, *, seq, bb, ts, prompt):
    m = x.shape[0]
    tm = bb * ts
    nst = seq // ts
    row = lambda i: (i, 0)
    if prompt:
        modmap = lambda k: (lambda i: (l, mod_row0 + i // nst, 0, k))
        tabmap = lambda i: (0, i % nst, 0)
    else:
        modmap = lambda k: (lambda i: (l, mod_row0 + i, 0, k))
        tabmap = lambda i: (0, 0, 0)
    in_specs = [pl.BlockSpec((tm, D_MODEL), row),
                pl.BlockSpec((None, bb, 1, D_MODEL), modmap(0)),
                pl.BlockSpec((None, bb, 1, D_MODEL), modmap(1)),
                pl.BlockSpec((2, ts, LANES), tabmap),
                _layer_spec((1, D_MODEL), l), _layer_spec((1, Q_LORA), l), _layer_spec((1, KV_LORA), l),
                _layer_spec((D_MODEL, IN_END), l)]
    args = [x, mod, mod, tab, wts["g_attn"], wts["g_q"], wts["g_kv"], wts["w_in"]]
    out_specs = [pl.BlockSpec((tm, Q_LORA), row), pl.BlockSpec((tm, LATENT), row),
                 pl.BlockSpec((tm, D_MODEL), row), pl.BlockSpec((tm, 2 * D_MODEL), row)]
    out_shape = [jax.ShapeDtypeStruct((m, Q_LORA), BF16), jax.ShapeDtypeStruct((m, LATENT), F32),
                 jax.ShapeDtypeStruct((m, D_MODEL), F32),
                 jax.ShapeDtypeStruct((m, 2 * D_MODEL), BF16 if prompt else F32)]
    if prompt:
        in_specs += [_layer_spec((KV_LORA, N_HEADS * HEAD_BLOCK), l)] * 2
        args += [wts["w_k"], wts["w_v"]]
        out_specs += [pl.BlockSpec((tm, N_HEADS * HEAD_BLOCK), row)] * 2
        out_shape += [jax.ShapeDtypeStruct((m, N_HEADS * HEAD_BLOCK), BF16)] * 2
    return pl.pallas_call(
        functools.partial(_in_proj_kernel, bb=bb, ts=ts, prompt=prompt),
        grid=(m // tm,), in_specs=in_specs, out_specs=out_specs, out_shape=out_shape,
        compiler_params=_params("parallel"),
        name="in_proj_prompt" if prompt else "in_proj_sample",
    )(*args)


def _q_proj_kernel(cqn_ref, tab_ref, wq_ref, q_ref, *, bb, ts):
    q = _dot(cqn_ref[...], wq_ref[...])
    t0, t1 = (_table_rows(tab_ref, i, bb, ts) for i in range(2))
    for hd in range(N_HEADS):
        sl = slice(hd * HEAD_BLOCK, (hd + 1) * HEAD_BLOCK)
        q_ref[:, sl] = _rope_block(q[:, sl], t0, t1).astype(BF16)


def _q_proj(cqn, tab, w_q, l, *, seq, bb, ts, prompt):
    m = cqn.shape[0]
    tm = bb * ts
    nst = seq // ts
    tabmap = (lambda i: (0, i % nst, 0)) if prompt else (lambda i: (0, 0, 0))
    return pl.pallas_call(
        functools.partial(_q_proj_kernel, bb=bb, ts=ts),
        grid=(m // tm,),
        in_specs=[pl.BlockSpec((tm, Q_LORA), lambda i: (i, 0)),
                  pl.BlockSpec((2, ts, LANES), tabmap),
                  _layer_spec((Q_LORA, N_HEADS * HEAD_BLOCK), l)],
        out_specs=pl.BlockSpec((tm, N_HEADS * HEAD_BLOCK), lambda i: (i, 0)),
        out_shape=jax.ShapeDtypeStruct((m, N_HEADS * HEAD_BLOCK), BF16),
        compiler_params=_params("parallel"),
        name="q_proj_prompt" if prompt else "q_proj_sample",
    )(cqn, tab, w_q)


HEADS_PER_STEP = 2


ATTN_STRIP = 16


def _attn_groups(nq):
    if nq % 2:
        return [(g,) for g in range(nq)]
    return [(g, nq - 1 - g) for g in range(nq // 2)]


def _attn_kernel(q_ref, k_ref, v_ref, o_ref, s_ref, p_ref, m_ref, alpha_ref, acc_ref, *, tq, groups):
    out0 = {}

    def run_group(group):
        tiles = [(slot, qi, j) for slot, qi in enumerate(group) for j in range(qi + 1)]
        for t, (slot, qi, j) in enumerate(tiles):
            buf = t % 2
            first, last = j == 0, j == qi
            halves = 2 if last else 1
            hrows = tq // halves
            spans = [(slice(h * hrows, (h + 1) * hrows), (h + 1) * hrows) for h in range(halves)]
            for hh in range(HEADS_PER_STEP):
                hs = slice(hh * HEAD_BLOCK, (hh + 1) * HEAD_BLOCK)
                for rs, kw in spans:
                    s_ref[buf, hh, rs, 0:kw] = _dot_nt(q_ref[0, qi * tq + rs.start:qi * tq + rs.stop, hs],
                                                       k_ref[0, j * tq:j * tq + kw, hs])
            for hh in range(HEADS_PER_STEP):
                hs = slice(hh * HEAD_BLOCK, (hh + 1) * HEAD_BLOCK)
                for r in range(tq // ATTN_STRIP):
                    rows = slice(r * ATTN_STRIP, (r + 1) * ATTN_STRIP)
                    kw = spans[r * ATTN_STRIP // hrows][1]
                    s = s_ref[buf, hh, rows, 0:kw]
                    if last:
                        qpos = r * ATTN_STRIP + lax.broadcasted_iota(jnp.int32, (ATTN_STRIP, kw), 0)
                        kpos = lax.broadcasted_iota(jnp.int32, (ATTN_STRIP, kw), 1)
                        s = jnp.where(kpos <= qpos, s, NEG_INF)
                    mx = jnp.max(s, axis=-1, keepdims=True)
                    if first:
                        m_new = jnp.broadcast_to(mx, (ATTN_STRIP, LANES))
                    else:
                        m_prev = m_ref[slot, hh, rows, :]
                        m_new = jnp.maximum(m_prev, mx)
                        alpha_ref[buf, hh, rows, :] = jnp.exp2(m_prev - m_new)
                    if not last:
                        m_ref[slot, hh, rows, :] = m_new
                    p_ref[buf, hh, rows, 0:kw] = jnp.exp2(s - jnp.tile(m_new, (1, kw // LANES))).astype(BF16)
                for rs, kw in spans:
                    pv = _dot(p_ref[buf, hh, rs, 0:kw], v_ref[0, j * tq:j * tq + kw, hs])
                    acc = pv if first else alpha_ref[buf, hh, rs, :] * acc_ref[slot, hh, rs, :] + pv
                    if last:
                        out = acc / acc[:, V_HEAD:V_HEAD + 1]
                        if hh == 0:
                            out0[rs.start] = out
                        else:
                            lane = lax.broadcasted_iota(jnp.int32, out.shape, 1)
                            both = jnp.where(lane < V_HEAD, out0[rs.start], pltpu.roll(out, V_HEAD, 1))
                            o_ref[0, qi * tq + rs.start:qi * tq + rs.stop, :] = both.astype(BF16)
                    else:
                        acc_ref[slot, hh, rs, :] = acc

    if len(groups) == 1:
        run_group(groups[0])
    else:
        for g, group in enumerate(groups):
            pl.when(pl.program_id(2) == g)(functools.partial(run_group, group))


def _attn(q, k, v, *, tq):
    b, s, _ = q.shape
    hw = HEADS_PER_STEP * HEAD_BLOCK
    vw = HEADS_PER_STEP * V_HEAD
    groups = _attn_groups(s // tq)
    slots = len(groups[0])
    seq = pl.BlockSpec((1, s, hw), lambda bi, hp, g: (bi, 0, hp))
    return pl.pallas_call(
        functools.partial(_attn_kernel, tq=tq, groups=groups),
        grid=(b, N_HEADS // HEADS_PER_STEP, len(groups)),
        in_specs=[seq, seq, seq],
        out_specs=pl.BlockSpec((1, s, vw), lambda bi, hp, g: (bi, 0, hp)),
        out_shape=jax.ShapeDtypeStruct((b, s, N_HEADS * V_HEAD), BF16),
        scratch_shapes=[pltpu.VMEM((2, HEADS_PER_STEP, tq, tq), F32), pltpu.VMEM((2, HEADS_PER_STEP, tq, tq), BF16),
                        pltpu.VMEM((slots, HEADS_PER_STEP, tq, LANES), F32),
                        pltpu.VMEM((2, HEADS_PER_STEP, tq, LANES), F32),
                        pltpu.VMEM((slots, HEADS_PER_STEP, tq, HEAD_BLOCK), F32)],
        compiler_params=_params("parallel", "parallel", "arbitrary"),
        name="attn_prompt",
    )(q, k, v)


def _absorb_kernel(q_ref, w_ref, qlat_ref, qrope_ref, *, n_seq, ts):
    blk = q_ref[...]
    qlat = _dot(blk, w_ref[0])
    qlat_ref[...] = qlat.reshape(n_seq, ts, KV_LORA)
    rolled = pltpu.roll(blk.astype(F32), LANES - QK_NOPE, 1)
    qrope_ref[...] = rolled[:, 0:QK_ROPE].reshape(n_seq, ts, QK_ROPE)


def _absorb(q, w_ukt, l, *, n_seq, ts):
    m = q.shape[0]
    return pl.pallas_call(
        functools.partial(_absorb_kernel, n_seq=n_seq, ts=ts),
        grid=(N_HEADS,),
        in_specs=[pl.BlockSpec((m, HEAD_BLOCK), lambda h: (0, h)),
                  pl.BlockSpec((None, 1, HEAD_BLOCK, KV_LORA), lambda h: (l, h, 0, 0))],
        out_specs=[pl.BlockSpec((n_seq, None, ts, KV_LORA), lambda h: (0, h, 0, 0)),
                   pl.BlockSpec((n_seq, None, ts, QK_ROPE), lambda h: (0, h, 0, 0))],
        out_shape=[jax.ShapeDtypeStruct((n_seq, N_HEADS, ts, KV_LORA), F32),
                   jax.ShapeDtypeStruct((n_seq, N_HEADS, ts, QK_ROPE), F32)],
        compiler_params=_params("parallel"),
        name="absorb",
    )(q, w_ukt)


PAGES_PER_CHUNK = 16
DMA_PRIORITIES = 2


def _paged_attn_kernel(pt_ref, qlat_ref, qrope_ref, new_ref, cache_ref, o_ref, buf_ref, kt_ref, s_ref, sem,
                       *, n_pages, ts, layer):
    b = pl.program_id(0)
    nq = N_HEADS * ts
    slot = b % 2

    def page_copy(seq, j, slot_):
        page = pt_ref[seq * n_pages + j]
        return pltpu.make_async_copy(cache_ref.at[layer, page], buf_ref.at[slot_, j], sem.at[slot_])

    @pl.when(b == 0)
    def _():
        for j in range(n_pages):
            page_copy(0, j, 0).start(priority=j % DMA_PRIORITIES)

    @pl.when(b + 1 < pl.num_programs(0))
    def _():
        for j in range(n_pages):
            page_copy(b + 1, j, 1 - slot).start(priority=j % DMA_PRIORITIES)

    for j in range(n_pages):
        page_copy(b, j, slot).wait()
    ql = qlat_ref[0].reshape(nq, KV_LORA).astype(BF16)
    qr = qrope_ref[0].reshape(nq, QK_ROPE).astype(BF16)
    new = new_ref[0]
    nl = new[:, 0:KV_LORA].astype(BF16)
    s_new = _dot_nt(ql, nl) + _dot_nt(qr, new[:, KV_LORA:LATENT].astype(BF16))
    tok = lax.broadcasted_iota(jnp.int32, (nq, ts), 0) % ts
    key = lax.broadcasted_iota(jnp.int32, (nq, ts), 1)
    s_new = jnp.where(key <= tok, s_new, NEG_INF)
    m = jnp.max(s_new, axis=-1, keepdims=True)
    chunk_pages = min(n_pages, PAGES_PER_CHUNK)
    chunks = [slice(c * chunk_pages * PAGE_SIZE, (c + 1) * chunk_pages * PAGE_SIZE)
              for c in range(n_pages // chunk_pages)]
    for c, cols in enumerate(chunks):
        for j in range(c * chunk_pages, (c + 1) * chunk_pages):
            kt_ref[:, j * PAGE_SIZE:(j + 1) * PAGE_SIZE] = buf_ref[slot, j].astype(BF16)
        s = _dot(ql, kt_ref[0:KV_LORA, cols]) + _dot(qr, kt_ref[KV_LORA:LATENT, cols])
        s_ref[:, cols] = s
        m = jnp.maximum(m, jnp.max(s, axis=-1, keepdims=True))
    p_new = jnp.exp2(s_new - m)
    l = jnp.sum(p_new, axis=-1, keepdims=True)
    acc = _dot(p_new.astype(BF16), nl)
    for cols in chunks:
        p = jnp.exp2(s_ref[:, cols] - m)
        l = l + jnp.sum(p, axis=-1, keepdims=True)
        acc = acc + _dot_nt(p.astype(BF16), kt_ref[0:KV_LORA, cols])
    o = acc / l
    o_ref[0] = o.reshape(N_HEADS, ts, KV_LORA)


def _paged_attn(page_table, qlat, qrope, lat_new, cache_t, layer):
    n_seq, n_pages = page_table.shape
    ts = qlat.shape[2]

    grid_spec = pltpu.PrefetchScalarGridSpec(
        num_scalar_prefetch=1,
        grid=(n_seq,),
        in_specs=[pl.BlockSpec((1, N_HEADS, ts, KV_LORA), lambda b, pt: (b, 0, 0, 0)),
                  pl.BlockSpec((1, N_HEADS, ts, QK_ROPE), lambda b, pt: (b, 0, 0, 0)),
                  pl.BlockSpec((1, ts, LATENT), lambda b, pt: (b, 0, 0)),
                  pl.BlockSpec(memory_space=pltpu.HBM)],
        out_specs=pl.BlockSpec((1, N_HEADS, ts, KV_LORA), lambda b, pt: (b, 0, 0, 0)),
        scratch_shapes=[pltpu.VMEM((2, n_pages, LATENT, PAGE_SIZE), F32),
                        pltpu.VMEM((LATENT, n_pages * PAGE_SIZE), BF16),
                        pltpu.VMEM((N_HEADS * ts, n_pages * PAGE_SIZE), F32),
                        pltpu.SemaphoreType.DMA((2,))],
    )
    return pl.pallas_call(
        functools.partial(_paged_attn_kernel, n_pages=n_pages, ts=ts, layer=layer),
        grid_spec=grid_spec,
        out_shape=jax.ShapeDtypeStruct((n_seq, N_HEADS, ts, KV_LORA), F32),
        compiler_params=_params("arbitrary"),
        name="paged_attn",
    )(page_table.reshape(-1), qlat, qrope, lat_new, cache_t)


def _v_up_kernel(o_ref, w_ref, out_ref, *, n_seq, ts):
    acc = None
    for e in range(HEADS_PER_STEP):
        x = o_ref[:, e].reshape(n_seq * ts, KV_LORA).astype(BF16)
        d = _dot(x, w_ref[e])
        acc = d if acc is None else acc + d
    out_ref[...] = acc


def _v_up(olat, w_vup, l):
    n_seq, _, ts, _ = olat.shape
    vw = HEADS_PER_STEP * V_HEAD
    return pl.pallas_call(
        functools.partial(_v_up_kernel, n_seq=n_seq, ts=ts),
        grid=(N_HEADS // HEADS_PER_STEP,),
        in_specs=[pl.BlockSpec((n_seq, HEADS_PER_STEP, ts, KV_LORA), lambda p: (0, p, 0, 0)),
                  pl.BlockSpec((None, HEADS_PER_STEP, KV_LORA, vw), lambda p: (l, p, 0, 0))],
        out_specs=pl.BlockSpec((n_seq * ts, vw), lambda p: (0, p)),
        out_shape=jax.ShapeDtypeStruct((n_seq * ts, N_HEADS * V_HEAD), F32),
        compiler_params=_params("parallel"),
        name="v_up",
    )(olat, w_vup)


def _mix_kernel(x_ref, u_ref, prev_ref, gate_ref, o_ref, g1_ref, wpool_ref, pscale_ref, wout_ref,
                x1_ref, ext_ref, *, bb, ts, pos0, prompt):
    j = pl.program_id(1)
    tm = bb * ts
    u = u_ref[...]
    ext_ref[:, POOL_HALO:, :] = u
    if prompt:
        ext_ref[:, 0:POOL_HALO, :] = jnp.where(j > 0, prev_ref[...], 0.0)
    else:
        for t in range(POOL_CTX):
            ext_ref[:, POOL_HALO - POOL_CTX + t, :] = prev_ref[t]
    pos = pos0 + j * ts + lax.broadcasted_iota(jnp.int32, (1, ts, 1), 1)
    mixed = []
    for g, w in enumerate(POOL_WINDOWS):
        cs = slice(g * POOL_GW, (g + 1) * POOL_GW)
        acc = ext_ref[:, POOL_HALO:POOL_HALO + ts, cs]
        for t in range(1, w):
            acc = acc + ext_ref[:, POOL_HALO - t:POOL_HALO - t + ts, cs]
        cnt = jnp.minimum(pos + 1, w).astype(F32)
        y = acc / cnt - u[:, :, cs]
        z = _dot(y.reshape(tm, POOL_GW).astype(BF16), wpool_ref[g]) * pscale_ref[:, cs]
        ga = gate_ref[:, :, cs].astype(F32).reshape(tm, POOL_GW)
        gb = gate_ref[:, :, D_MODEL + g * POOL_GW:D_MODEL + (g + 1) * POOL_GW].astype(F32).reshape(tm, POOL_GW)
        mixed.append((ga * z + gb * o_ref[:, :, cs].astype(F32).reshape(tm, POOL_GW)).astype(BF16))
    out = _dot(jnp.concatenate(mixed, axis=-1), wout_ref[...])
    x1 = x_ref[...].reshape(tm, D_MODEL) + _rows(g1_ref, bb, ts) * out
    x1_ref[...] = x1.reshape(bb, ts, D_MODEL)


def _mix(x, u, prev, gate, o, mod, mod_row0, wts, l, *, bb, ts, pos0, prompt):
    b, s, _ = x.shape
    tok = lambda n: pl.BlockSpec((bb, ts, n), lambda bi, j: (bi, j, 0))
    if prompt:
        r = ts // POOL_HALO
        prev_spec = pl.BlockSpec((bb, POOL_HALO, D_MODEL), lambda bi, j: (bi, jnp.maximum(j * r - 1, 0), 0))
    else:
        prev_spec = pl.BlockSpec((None, POOL_CTX, bb, D_MODEL), lambda bi, j: (l, 0, bi, 0))
    return pl.pallas_call(
        functools.partial(_mix_kernel, bb=bb, ts=ts, pos0=pos0, prompt=prompt),
        grid=(b // bb, s // ts),
        in_specs=[tok(D_MODEL), tok(D_MODEL), prev_spec, tok(2 * D_MODEL), tok(D_MODEL),
                  pl.BlockSpec((None, bb, 1, D_MODEL), lambda bi, j: (l, mod_row0 + bi, 0, 2)),
                  _layer_spec((len(POOL_WINDOWS), POOL_GW, POOL_GW), l), _layer_spec((1, D_MODEL), l),
                  _layer_spec((D_MODEL, D_MODEL), l)],
        out_specs=tok(D_MODEL),
        out_shape=jax.ShapeDtypeStruct((b, s, D_MODEL), F32),
        scratch_shapes=[pltpu.VMEM((bb, POOL_HALO + ts, D_MODEL), F32)],
        compiler_params=_params("parallel", "arbitrary"),
        name="mix_prompt" if prompt else "mix_sample",
    )(x, u, prev, gate, o, mod, wts["w_pool"], wts["pool_scale"], wts["w_out"])


MXU_DIM = 256
FF_CHUNKS = (6 * MXU_DIM, 5 * MXU_DIM)
FF_STARTS = tuple(sum(FF_CHUNKS[:c]) for c in range(len(FF_CHUNKS)))
assert sum(FF_CHUNKS) == D_FF
CONV_HALO = SUBLANES


def _ffn_kernel(x_ref, prev_ref, sh_ref, sc_ref, g2_ref, gf_ref, wup_ref, wconv_ref, bconv_ref, wdown_ref,
                gfin_ref, y_ref, cnew_ref, extg_ref, extv_ref, *, bb, ts, prompt, final):
    j = pl.program_id(1)
    tm = bb * ts
    x = x_ref[...].reshape(tm, D_MODEL)
    sc = _rows(sc_ref, bb, ts)
    sh = _rows(sh_ref, bb, ts)
    if prompt:
        xe = jnp.concatenate([prev_ref[0], x], axis=0)
    else:
        xe = x
    hb = (_rms(xe, gf_ref[...]) * (1.0 + sc) + sh).astype(BF16)
    acc = jnp.zeros((tm, D_MODEL), F32)
    for start, width in zip(FF_STARTS, FF_CHUNKS):
        conv = []
        for part, ext_ref in ((0, extg_ref), (1, extv_ref)):
            cs = slice(part * D_FF + start, part * D_FF + start + width)
            z = _dot(hb, wup_ref[:, cs])
            if prompt:
                keep = jnp.logical_or(j > 0, lax.broadcasted_iota(jnp.int32, (tm + CONV_HALO, 1), 0) >= CONV_HALO)
                ext_ref[0, :, 0:width] = jnp.where(keep, z, 0.0)
            else:
                ext_ref[:, CONV_HALO:, 0:width] = z.reshape(bb, ts, width)
                ext_ref[:, CONV_HALO - (CONV_W - 1):CONV_HALO, 0:width] = prev_ref[:, :, cs]
            out = bconv_ref[:, cs][None]
            for t in range(CONV_W):
                lo_r = CONV_HALO - (CONV_W - 1) + t
                out = out + ext_ref[:, lo_r:lo_r + ts, 0:width] * wconv_ref[t:t + 1, cs][None]
            cnew_ref[:, :, cs] = ext_ref[:, CONV_HALO + ts - (CONV_W - 1):CONV_HALO + ts, 0:width]
            conv.append(out.reshape(tm, width))
        act = (jax.nn.silu(conv[0]) * conv[1]).astype(BF16)
        acc = acc + _dot(act, wdown_ref[start:start + width, :])
    x2 = x + _rows(g2_ref, bb, ts) * acc
    if final:
        x2 = _rms(x2, gfin_ref[...])
    y_ref[...] = x2.reshape(bb, ts, D_MODEL)


def _ffn(x, prev, mod, mod_row0, wts, l, g_final, *, bb, ts, prompt, final):
    b, s, _ = x.shape
    tok = pl.BlockSpec((bb, ts, D_MODEL), lambda bi, j: (bi, j, 0))
    if prompt:
        r = ts // CONV_HALO
        prev_spec = pl.BlockSpec((bb, CONV_HALO, D_MODEL), lambda bi, j: (bi, jnp.maximum(j * r - 1, 0), 0))
    else:
        prev_spec = pl.BlockSpec((None, bb, CONV_W - 1, 2 * D_FF), lambda bi, j: (l, bi, 0, 0))
    modspec = lambda k: pl.BlockSpec((None, bb, 1, D_MODEL), lambda bi, j: (l, mod_row0 + bi, 0, k))
    return pl.pallas_call(
        functools.partial(_ffn_kernel, bb=bb, ts=ts, prompt=prompt, final=final),
        grid=(b // bb, s // ts),
        in_specs=[tok, prev_spec, modspec(3), modspec(4), modspec(5), _layer_spec((1, D_MODEL), l),
                  _layer_spec((D_MODEL, 2 * D_FF), l), _layer_spec((CONV_W, 2 * D_FF), l),
                  _layer_spec((1, 2 * D_FF), l), _layer_spec((D_FF, D_MODEL), l), _const_spec((1, D_MODEL))],
        out_specs=[tok, pl.BlockSpec((bb, CONV_W - 1, 2 * D_FF), lambda bi, j: (bi, 0, 0))],
        out_shape=[jax.ShapeDtypeStruct((b, s, D_MODEL), F32),
                   jax.ShapeDtypeStruct((b, CONV_W - 1, 2 * D_FF), F32)],
        scratch_shapes=[pltpu.VMEM((bb, CONV_HALO + ts, max(FF_CHUNKS)), F32),
                        pltpu.VMEM((bb, CONV_HALO + ts, max(FF_CHUNKS)), F32)],
        compiler_params=_params("parallel", "arbitrary"),
        name=("ffn_prompt" if prompt else "ffn_sample") + ("_final" if final else ""),
    )(x, prev, mod, mod, mod, wts["g_ffn"], wts["w_up"], wts["w_conv"], wts["b_conv"], wts["w_down"], g_final)


def _rope_tables(pos):
    freq = ROPE_THETA ** (-jnp.arange(ROPE_HALF, dtype=F32) / ROPE_HALF)
    ang = pos[:, None] * freq[None, :]
    cos, sin = jnp.cos(ang), jnp.sin(ang)
    z16 = jnp.zeros_like(cos)
    one = jnp.ones((pos.shape[0], QK_NOPE), F32)
    zero = jnp.zeros((pos.shape[0], QK_NOPE), F32)
    span0, span1 = [cos, cos, z16, z16], [-sin, sin, z16, z16]
    cat = lambda parts: jnp.concatenate(parts, axis=-1)
    tab_k = jnp.stack([cat(span0 + span0), cat(span1 + span1)])
    tab_q = jnp.stack([cat([one] + span0), cat([zero] + span1)])
    return tab_k, tab_q * Q_PRESCALE


def _prep_weights(w_in, g_attn, g_q, w_uq, g_kv, w_uk, w_uv, w_pool, pool_scale, w_out, g_ffn, w_up, w_conv, b_conv,
                  w_down):
    depth = w_in.shape[0]
    o2 = IN_A
    o3 = o2 + QK_ROPE
    kr = w_in[:, :, o2:o3]
    span = [kr, kr[:, :, :ROPE_HALF], jnp.zeros((depth, D_MODEL, ROPE_HALF), F32)]
    w_in_p = jnp.concatenate([w_in[:, :, :o2]] + span + span + [w_in[:, :, o3:]], axis=2).astype(BF16)
    wq = w_uq.reshape(depth, Q_LORA, N_HEADS, QK_NOPE + QK_ROPE)
    w_q = jnp.concatenate([wq, wq[..., QK_NOPE:QK_NOPE + ROPE_HALF],
                           jnp.zeros((depth, Q_LORA, N_HEADS, ROPE_HALF), F32)], axis=-1)
    w_k = jnp.pad(w_uk, ((0, 0), (0, 0), (0, 0), (0, HEAD_BLOCK - QK_NOPE)))
    w_v = jnp.pad(w_uv, ((0, 0), (0, 0), (0, 0), (0, HEAD_BLOCK - V_HEAD)))
    w_ukt = jnp.pad(jnp.transpose(w_uk, (0, 2, 3, 1)), ((0, 0), (0, 0), (0, HEAD_BLOCK - QK_NOPE), (0, 0)))
    wv_heads = jnp.transpose(w_uv, (0, 2, 1, 3))
    zeros_v = jnp.zeros_like(wv_heads)
    even = (jnp.arange(N_HEADS) % HEADS_PER_STEP == 0)[None, :, None, None]
    w_vup = jnp.concatenate([jnp.where(even, wv_heads, zeros_v), jnp.where(even, zeros_v, wv_heads)], axis=-1)
    return dict(
        w_in=w_in_p, g_attn=g_attn[:, None], g_q=g_q[:, None], g_kv=g_kv[:, None],
        w_q=w_q.reshape(depth, Q_LORA, N_HEADS * HEAD_BLOCK).astype(BF16),
        w_k=w_k.reshape(depth, KV_LORA, N_HEADS * HEAD_BLOCK).astype(BF16),
        w_v=w_v.reshape(depth, KV_LORA, N_HEADS * HEAD_BLOCK).astype(BF16),
        w_ukt=w_ukt.astype(BF16), w_vup=w_vup.astype(BF16),
        w_pool=w_pool.astype(BF16), pool_scale=pool_scale[:, None], w_out=w_out.astype(BF16),
        g_ffn=g_ffn[:, None], w_up=w_up.astype(BF16), w_conv=w_conv,
        b_conv=b_conv[:, None], w_down=w_down.astype(BF16))


def _pick(n, pref):
    t = min(n, pref)
    while n % t:
        t -= 1
    return t


def kernel(x_prompt, x_sample, cache_latent, state_pool, state_conv, page_table, c_prompt, c_sample, w_ada, b_ada, g_attn, w_in, g_q, w_uq, g_kv, w_uk, w_uv, w_pool, pool_scale, w_out, g_ffn, w_up, w_conv, b_conv, w_down, g_final):
    depth = w_ada.shape[0]
    bp, sp, _ = x_prompt.shape
    bs, ss, _ = x_sample.shape
    past_len = page_table.shape[1] * PAGE_SIZE

    ts_p = _pick(sp, 512)
    bb_s = _pick(bs, 32)
    mod = _ada(jnp.concatenate([c_sample, c_prompt], axis=0), w_ada.astype(BF16), b_ada[:, None, :])
    mod = mod.reshape(depth, bs + bp, 1, -1)
    row0_s, row0_p = 0, bs
    tabk_p, tabq_p = _rope_tables(jnp.arange(sp, dtype=F32))
    tabk_s, tabq_s = _rope_tables(past_len + jnp.arange(ss, dtype=F32))
    gfin = g_final[None]
    cache_t = jnp.swapaxes(cache_latent, 2, 3)
    pool_t = jnp.swapaxes(state_pool, 1, 2)
    wts = _prep_weights(w_in, g_attn, g_q, w_uq, g_kv, w_uk, w_uv, w_pool, pool_scale, w_out, g_ffn, w_up, w_conv,
                        b_conv, w_down)

    yp, ys = x_prompt, x_sample
    outs = [[] for _ in range(6)]
    for l in range(depth):
        final = l == depth - 1

        cqn, lat, u, gate, kf, vf = _in_proj(yp.reshape(bp * sp, D_MODEL), mod, row0_p, tabk_p, wts, l,
                                             seq=sp, bb=1, ts=ts_p, prompt=True)
        q = _q_proj(cqn, tabq_p, wts["w_q"], l, seq=sp, bb=1, ts=ts_p, prompt=True)
        o = _attn(q.reshape(bp, sp, -1), kf.reshape(bp, sp, -1), vf.reshape(bp, sp, -1), tq=ts_p)
        u3 = u.reshape(bp, sp, D_MODEL)
        x1 = _mix(yp, u3, u3, gate.reshape(bp, sp, -1), o, mod, row0_p, wts, l,
                  bb=1, ts=ts_p, pos0=0, prompt=True)
        yp, conv_p = _ffn(x1, x1, mod, row0_p, wts, l, gfin, bb=1, ts=ts_p, prompt=True, final=final)
        outs[0].append(lat.reshape(bp, sp, LATENT))
        outs[1].append(u3[:, sp - POOL_CTX:, :])
        outs[2].append(conv_p)

        cqn, lat, u, gate = _in_proj(ys.reshape(bs * ss, D_MODEL), mod, row0_s, tabk_s, wts, l,
                                     seq=ss, bb=bb_s, ts=ss, prompt=False)
        q = _q_proj(cqn, tabq_s, wts["w_q"], l, seq=ss, bb=bb_s, ts=ss, prompt=False)
        qlat, qrope = _absorb(q, wts["w_ukt"], l, n_seq=bs, ts=ss)
        lat3 = lat.reshape(bs, ss, LATENT)
        olat = _paged_attn(page_table, qlat, qrope, lat3, cache_t, l)
        o = _v_up(olat, wts["w_vup"], l)
        u3 = u.reshape(bs, ss, D_MODEL)
        x1 = _mix(ys, u3, pool_t, gate.reshape(bs, ss, -1), o.reshape(bs, ss, -1), mod, row0_s, wts, l,
                  bb=bb_s, ts=ss, pos0=past_len, prompt=False)
        ys, conv_s = _ffn(x1, state_conv, mod, row0_s, wts, l, gfin, bb=bb_s, ts=ss, prompt=False, final=final)
        outs[3].append(lat3)
        outs[4].append(jnp.concatenate([state_pool[l], u3], axis=1)[:, -POOL_CTX:, :])
        outs[5].append(conv_s)

    return (yp, ys) + tuple(jnp.stack(o) for o in outs)
```
